```python
import jax, jax.numpy as jnp
from jax import lax
import numpy as np

D_MODEL = 1024
BATCH = 8
SEQ = 2048
DEPTH = 4

CHUNK = 64
Q_BLOCK = 128
EPS = 1e-6

BRANCH_WIDTH = D_MODEL // 2
N_BRANCH = 4

A_HEADS = 4
A_DK = (BRANCH_WIDTH // 2) // A_HEADS
A_DV = BRANCH_WIDTH // A_HEADS
A_RANK = 16
A_GATE_NORM = 16.0

B_HEADS = 8
B_DH = BRANCH_WIDTH // B_HEADS
B_PREV_CHUNKS = 8
B_MAX_REL = 128

C_WIDTH = BRANCH_WIDTH
C_BLOCKS = 8
C_BLOCK_DIM = C_WIDTH // C_BLOCKS
C_CONV = 4
C_POW = 8.0

D_HEADS = 8
D_DH = BRANCH_WIDTH // D_HEADS

FFN_HIDDEN = -(-8 * D_MODEL // (3 * 256)) * 256

IN_SIZES = (
    A_HEADS * A_DK, A_HEADS * A_DK, A_HEADS * A_DV, A_RANK, A_HEADS * A_DV,
    B_HEADS * B_DH, B_HEADS * B_DH, B_HEADS * B_DH,
    C_WIDTH, C_WIDTH,
    D_HEADS * D_DH, D_HEADS * D_DH, D_HEADS * D_DH,
    N_BRANCH * D_MODEL,
)
IN_TOTAL = int(sum(IN_SIZES))
SPLIT_POINTS = tuple(int(v) for v in np.cumsum(IN_SIZES)[:-1])

kernel_name = 'hybrid_chunk_causal_parallel_mixer_trunk'

F32 = jnp.float32


def rmsnorm(x, g):
    xf = x.astype(F32)
    y = xf * lax.rsqrt(jnp.mean(xf * xf, axis=-1, keepdims=True) + EPS) * g.astype(F32)
    return y.astype(x.dtype)


def gla_mixer(q, k, v, r, g, w_gk, b_gk, norm_g):
    bsz, s, _ = q.shape
    nc = s // CHUNK
    gk = jax.nn.log_sigmoid((r @ w_gk + b_gk).astype(F32)) / A_GATE_NORM

    def chunks(t, d):
        return t.astype(F32).reshape(bsz, nc, CHUNK, A_HEADS, d).transpose(1, 0, 2, 3, 4)

    qc = chunks(q, A_DK) * (A_DK ** -0.5)
    kc = chunks(k, A_DK)
    vc = chunks(v, A_DV)
    cum = jnp.cumsum(chunks(gk, A_DK), axis=2)
    tot = cum[:, :, -1]
    k_dec = kc * jnp.exp(tot[:, :, None] - cum)

    def step(state, xs):
        q_c, kd_c, v_c, tot_c = xs
        state = jnp.exp(tot_c)[..., None] * state + jnp.einsum('bthk,bthv->bhkv', kd_c, v_c)
        return state, jnp.einsum('bthk,bhkv->bthv', q_c, state)

    state0 = jnp.zeros((bsz, A_HEADS, A_DK, A_DV), F32)
    _, o = lax.scan(step, state0, (qc, k_dec, vc, tot))
    o = o.transpose(1, 0, 2, 3, 4).reshape(bsz, s, A_HEADS, A_DV)
    o = rmsnorm(o, norm_g).reshape(bsz, s, A_HEADS * A_DV)
    return (o * jax.nn.silu(g.astype(F32))).astype(q.dtype)


def chunk_rel_attention(q, k, v, rel_table):
    bsz, s, _ = q.shape
    nc = s // CHUNK
    n_band = B_PREV_CHUNKS + 1
    band = n_band * CHUNK
    qc = q.reshape(bsz, nc, CHUNK, B_HEADS, B_DH)
    idx = jnp.arange(nc)[:, None] + jnp.arange(n_band)[None, :]

    def gather_band(t):
        tc = t.reshape(bsz, nc, CHUNK, B_HEADS, B_DH)
        tp = jnp.pad(tc, ((0, 0), (B_PREV_CHUNKS, 0), (0, 0), (0, 0), (0, 0)))
        return tp[:, idx].reshape(bsz, nc, band, B_HEADS, B_DH)

    kb = gather_band(k)
    vb = gather_band(v)
    sc = jnp.einsum('bnqhd,bnkhd->bhnqk', qc, kb).astype(F32) * (B_DH ** -0.5)
    qi = jnp.arange(CHUNK)[:, None]
    kj = jnp.arange(band)[None, :]
    rel = jnp.clip(B_PREV_CHUNKS * CHUNK + qi - kj, -B_MAX_REL, B_MAX_REL) + B_MAX_REL
    bias = rel_table.astype(F32)[:, rel]
    valid = (jnp.arange(nc)[:, None] - B_PREV_CHUNKS) * CHUNK + jnp.arange(band)[None, :] >= 0
    sc = jnp.where(valid[None, None, :, None, :], sc + bias[None, :, None], -1e30)
    p = jax.nn.softmax(sc, axis=-1)
    o = jnp.einsum('bhnqk,bnkhd->bnqhd', p.astype(v.dtype), vb)
    return o.reshape(bsz, s, B_HEADS * B_DH)


def rglru_mixer(gate_in, x_in, conv_w, conv_b, w_a, b_a, w_x, b_x, lam):
    bsz, s, _ = x_in.shape
    xp = jnp.pad(x_in, ((0, 0), (C_CONV - 1, 0), (0, 0)))
    xc = conv_b
    for j in range(C_CONV):
        xc = xc + xp[:, j:j + s] * conv_w[j]
    xg = xc.reshape(bsz, s, C_BLOCKS, C_BLOCK_DIM)
    r = jax.nn.sigmoid(jnp.einsum('bsgi,gij->bsgj', xg, w_a).reshape(bsz, s, C_WIDTH) + b_a)
    i = jax.nn.sigmoid(jnp.einsum('bsgi,gij->bsgj', xg, w_x).reshape(bsz, s, C_WIDTH) + b_x)
    log_a = -C_POW * r.astype(F32) * jax.nn.softplus(-lam.astype(F32))
    a = jnp.exp(log_a)
    bx = jnp.sqrt(-jnp.expm1(2.0 * log_a)) * (i * xc).astype(F32)

    def combine(left, right):
        a1, b1 = left
        a2, b2 = right
        return a1 * a2, a2 * b1 + b2

    _, h = lax.associative_scan(combine, (a, bx), axis=1)
    return jax.nn.gelu(gate_in) * h.astype(x_in.dtype)


def stick_breaking_attention(q, k, v):
    bsz, s, _ = q.shape
    qh = q.reshape(bsz, s, D_HEADS, D_DH)
    kh = k.reshape(bsz, s, D_HEADS, D_DH)
    vh = v.reshape(bsz, s, D_HEADS, D_DH)
    outs = []
    for blk in range(s // Q_BLOCK):
        end = (blk + 1) * Q_BLOCK
        z = jnp.einsum('bqhd,bkhd->bhqk', qh[:, blk * Q_BLOCK:end], kh[:, :end]).astype(F32) * (D_DH ** -0.5)
        qpos = blk * Q_BLOCK + jnp.arange(Q_BLOCK)[:, None]
        kpos = jnp.arange(end)[None, :]
        before = kpos < qpos
        log1m = jnp.where(before, jax.nn.log_sigmoid(-z), 0.0)
        tail = lax.cumsum(log1m, axis=3, reverse=True) - log1m
        att = jnp.where(before, jnp.exp(jax.nn.log_sigmoid(z) + tail), 0.0)
        outs.append(jnp.einsum('bhqk,bkhd->bqhd', att.astype(v.dtype), vh[:, :end]))
    return jnp.concatenate(outs, axis=1).reshape(bsz, s, D_HEADS * D_DH)


def setup_inputs(seed: int = 0) -> dict:
    key = jax.random.key(seed)
    ks = jax.random.split(key, 24)
    L = DEPTH

    def nrm(k, shape, scale):
        return jax.random.normal(k, shape, F32) * scale

    a0 = jax.random.uniform(ks[13], (L, C_WIDTH), F32, 0.9, 0.999)
    u = a0 ** (1.0 / C_POW)
    c_lambda = jnp.log(u) - jnp.log1p(-u)
    return {
        'x': nrm(ks[0], (BATCH, SEQ, D_MODEL), 1.0),
        'norm_mix': 1.0 + nrm(ks[1], (L, D_MODEL), 0.02),
        'w_in': nrm(ks[2], (L, D_MODEL, IN_TOTAL), D_MODEL ** -0.5),
        'a_w_gk': nrm(ks[3], (L, A_RANK, A_HEADS * A_DK), A_RANK ** -0.5),
        'a_b_gk': nrm(ks[4], (L, A_HEADS * A_DK), 0.1),
        'a_norm': 1.0 + nrm(ks[5], (L, A_DV), 0.02),
        'b_rel_bias': nrm(ks[6], (L, B_HEADS, 2 * B_MAX_REL + 1), 0.1),
        'c_conv_w': nrm(ks[7], (L, C_CONV, C_WIDTH), C_CONV ** -0.5),
        'c_conv_b': nrm(ks[8], (L, C_WIDTH), 0.01),
        'c_w_a': nrm(ks[9], (L, C_BLOCKS, C_BLOCK_DIM, C_BLOCK_DIM), C_BLOCK_DIM ** -0.5),
        'c_b_a': nrm(ks[10], (L, C_WIDTH), 0.01),
        'c_w_x': nrm(ks[11], (L, C_BLOCKS, C_BLOCK_DIM, C_BLOCK_DIM), C_BLOCK_DIM ** -0.5),
        'c_b_x': nrm(ks[12], (L, C_WIDTH), 0.01),
        'c_lambda': c_lambda,
        'w_branch': nrm(ks[14], (L, N_BRANCH, BRANCH_WIDTH, D_MODEL), BRANCH_WIDTH ** -0.5),
        'w_out': nrm(ks[15], (L, D_MODEL, D_MODEL), D_MODEL ** -0.5),
        'norm_ffn': 1.0 + nrm(ks[16], (L, D_MODEL), 0.02),
        'w_ffn_gate': nrm(ks[17], (L, D_MODEL, FFN_HIDDEN), D_MODEL ** -0.5),
        'w_ffn_up': nrm(ks[18], (L, D_MODEL, FFN_HIDDEN), D_MODEL ** -0.5),
        'w_ffn_down': nrm(ks[19], (L, FFN_HIDDEN, D_MODEL), FFN_HIDDEN ** -0.5),
        'norm_final': 1.0 + nrm(ks[20], (D_MODEL,), 0.02),
    }


def reference(x, norm_mix, w_in, a_w_gk, a_b_gk, a_norm, b_rel_bias, c_conv_w, c_conv_b,
              c_w_a, c_b_a, c_w_x, c_b_x, c_lambda, w_branch, w_out, norm_ffn,
              w_ffn_gate, w_ffn_up, w_ffn_down, norm_final):
    bsz, s, _ = x.shape
    h = x
    for l in range(DEPTH):
        xn = rmsnorm(h, norm_mix[l])
        proj = xn @ w_in[l]
        (a_q, a_k, a_v, a_r, a_g, b_q, b_k, b_v, c_g, c_x,
         d_q, d_k, d_v, gate_logits) = jnp.split(proj, SPLIT_POINTS, axis=-1)

        y_a = gla_mixer(a_q, a_k, a_v, a_r, a_g, a_w_gk[l], a_b_gk[l], a_norm[l])
        y_b = chunk_rel_attention(b_q, b_k, b_v, b_rel_bias[l])
        y_c = rglru_mixer(c_g, c_x, c_conv_w[l], c_conv_b[l], c_w_a[l], c_b_a[l],
                          c_w_x[l], c_b_x[l], c_lambda[l])
        y_d = stick_breaking_attention(d_q, d_k, d_v)

        branches = jnp.stack([y_a, y_b, y_c, y_d], axis=2)
        widened = jnp.einsum('bsnw,nwd->bsnd', branches, w_branch[l])
        gates = jax.nn.sigmoid(gate_logits).reshape(bsz, s, N_BRANCH, D_MODEL)
        mixed = jnp.sum(gates * widened, axis=2)
        h = h + mixed @ w_out[l]

        hn = rmsnorm(h, norm_ffn[l])
        h = h + (jax.nn.silu(hn @ w_ffn_gate[l]) * (hn @ w_ffn_up[l])) @ w_ffn_down[l]
    return rmsnorm(h, norm_final)
```

```python
import functools

import jax
import jax.numpy as jnp
from jax import lax
from jax.experimental import pallas as pl
from jax.experimental.pallas import tpu as pltpu

F32 = jnp.float32
BF16 = jnp.bfloat16

EPS = 1e-6
CHUNK = 64
BRANCH_WIDTH = 512
N_BRANCH = 4
HEAD_DIM = 64
A_HEADS = 4
A_DV = 128
A_RANK = 16
A_GATE_NORM = 16.0
B_HEADS = 8
B_PREV_CHUNKS = 8
B_MAX_REL = 128
C_BLOCKS = 8
C_CONV = 4
C_POW = 8.0
D_HEADS = 8
MASK_VALUE = -1e30

LANES = 128
SUBLANES = 8
VMEM_BYTES_V7X = 64 * 1024 * 1024

QKV_COLS = 7 * BRANCH_WIDTH
QKV_A_V, QKV_B_Q, QKV_B_K, QKV_B_V, QKV_D_Q, QKV_D_K, QKV_D_V = range(7)
GATE_COLS = N_BRANCH * 2 * BRANCH_WIDTH
F_A_G, F_C_G, F_C_X, F_A_QK = 8, 9, 10, 11
F_A_R_COL = 12 * BRANCH_WIDTH
F_COLS = F_A_R_COL + 2 * LANES

Q_BLOCK = 128
BAND = (B_PREV_CHUNKS + 2) * CHUNK
B_PAD = B_PREV_CHUNKS * CHUNK


def _dot(a, b):
    return jnp.dot(a, b, preferred_element_type=F32)


def _dot_nt(a, b):
    return lax.dot_general(a, b, (((1,), (1,)), ((), ())), preferred_element_type=F32)


def _dot_tn(a, b):
    return lax.dot_general(a, b, (((0,), (0,)), ((), ())), preferred_element_type=F32)


def _two_bf16_terms(x):
    hi = x.astype(BF16)
    lo = (x - hi.astype(F32)).astype(BF16)
    return hi, lo


def _log1p_exp_neg_abs(z):
    return jnp.log1p(jnp.exp(-jnp.abs(z)))


def _log_sigmoid(z):
    return jnp.minimum(z, 0.0) - _log1p_exp_neg_abs(z)


def _sigmoid(z):
    return 1.0 / (1.0 + jnp.exp(-z))


def _vmem_limit(pipelined_bytes, resident_bytes):
    want = 2 * pipelined_bytes + resident_bytes
    return int(min(max(want, 16 * 1024 * 1024), VMEM_BYTES_V7X - 8 * 1024 * 1024))


def _params(semantics, pipelined_bytes, resident_bytes):
    return pltpu.CompilerParams(
        dimension_semantics=semantics,
        vmem_limit_bytes=_vmem_limit(pipelined_bytes, resident_bytes))


def _norm_matmul_kernel(x_ref, g_ref, w_ref, o_ref, xn_ref):
    @pl.when(pl.program_id(1) == 0)
    def _():
        x = x_ref[...]
        inv = lax.rsqrt(jnp.mean(x * x, axis=-1, keepdims=True) + EPS)
        xn_ref[...] = (x * inv * g_ref[...]).astype(BF16)

    o_ref[...] = _dot(xn_ref[...], w_ref[...]).astype(o_ref.dtype)


def _norm_matmul(x, g, w, layer, out_dtype, tm, tn):
    t, d = x.shape
    n = w.shape[-1]
    out_bytes = jnp.dtype(out_dtype).itemsize
    return pl.pallas_call(
        _norm_matmul_kernel,
        grid=(t // tm, n // tn),
        in_specs=[
            pl.BlockSpec((tm, d), lambda i, j: (i, 0)),
            pl.BlockSpec((None, 1, d), lambda i, j: (layer, 0, 0)),
            pl.BlockSpec((None, d, tn), lambda i, j: (layer, 0, j)),
        ],
        out_specs=pl.BlockSpec((tm, tn), lambda i, j: (i, j)),
        out_shape=jax.ShapeDtypeStruct((t, n), out_dtype),
        scratch_shapes=[pltpu.VMEM((tm, d), BF16)],
        compiler_params=_params(
            ("parallel", "arbitrary"),
            tm * d * 4 + d * tn * 2 + tm * tn * out_bytes,
            tm * d * 2 + tm * d * 4 + tm * tn * 4),
        name="norm_proj",
    )(x, g, w)


def _gla_kernel(qk_ref, v_ref, g_ref, r_ref, wgk_ref, bgk_ref, ng_ref, o_ref, state_ref, *,
                n_chunks):
    dk_all = A_HEADS * HEAD_DIM

    @pl.when(pl.program_id(1) == 0)
    def _():
        state_ref[...] = jnp.zeros_like(state_ref)

    row = lax.broadcasted_iota(jnp.int32, (CHUNK, CHUNK), 0)
    col = lax.broadcasted_iota(jnp.int32, (CHUNK, CHUNK), 1)
    tri = jnp.where(row >= col, 1.0, 0.0).astype(BF16)
    lane_head = lax.broadcasted_iota(jnp.int32, (CHUNK, dk_all), 1) // HEAD_DIM

    for c in range(n_chunks):
        rows = slice(c * CHUNK, (c + 1) * CHUNK)
        r = r_ref[rows, :].astype(BF16)
        gk = _log_sigmoid(_dot(r, wgk_ref[...]) + bgk_ref[...]) * (1.0 / A_GATE_NORM)
        gk_hi, gk_lo = _two_bf16_terms(gk)
        cum = _dot(tri, gk_hi) + _dot(tri, gk_lo)
        tot = cum[CHUNK - 1:CHUNK, :]
        qk = qk_ref[rows, :]
        k_dec = (qk[:, dk_all:] * jnp.exp(tot - cum)).astype(BF16)
        q = qk[:, :dk_all] * (HEAD_DIM ** -0.5)
        kv = _dot_tn(v_ref[rows, :], k_dec)
        state = state_ref[...] * jnp.exp(tot) + kv
        state_ref[...] = state
        state_b = state.astype(BF16)
        for h in range(A_HEADS):
            cols = slice(h * A_DV, (h + 1) * A_DV)
            q_h = jnp.where(lane_head == h, q, 0.0).astype(BF16)
            o = _dot_nt(q_h, state_b[cols, :])
            o = o * lax.rsqrt(jnp.mean(o * o, axis=-1, keepdims=True) + EPS) * ng_ref[...]
            g = g_ref[rows, cols]
            o_ref[rows, cols] = (o * (g * _sigmoid(g))).astype(o_ref.dtype)


def _gla(pf, pj, wgk, bgk, ng, layer, tq):
    b, s, _ = pf.shape
    w = BRANCH_WIDTH
    return pl.pallas_call(
        functools.partial(_gla_kernel, n_chunks=tq // CHUNK),
        grid=(b, s // tq),
        in_specs=[
            pl.BlockSpec((None, tq, w), lambda bi, i: (bi, i, F_A_QK)),
            pl.BlockSpec((None, tq, w), lambda bi, i: (bi, i, QKV_A_V)),
            pl.BlockSpec((None, tq, w), lambda bi, i: (bi, i, F_A_G)),
            pl.BlockSpec((None, tq, LANES), lambda bi, i: (bi, i, F_A_R_COL // LANES)),
            pl.BlockSpec((None, LANES, A_HEADS * HEAD_DIM), lambda bi, i: (layer, 0, 0)),
            pl.BlockSpec((None, 1, A_HEADS * HEAD_DIM), lambda bi, i: (layer, 0, 0)),
            pl.BlockSpec((None, 1, A_DV), lambda bi, i: (layer, 0, 0)),
        ],
        out_specs=pl.BlockSpec((None, tq, w), lambda bi, i: (bi, i, 0)),
        out_shape=jax.ShapeDtypeStruct((b, s, w), BF16),
        scratch_shapes=[pltpu.VMEM((A_HEADS * A_DV, A_HEADS * HEAD_DIM), F32)],
        compiler_params=_params(("parallel", "arbitrary"), tq * w * 12 + tq * LANES * 4,
                                8 * 1024 * 1024),
        name="mixer_a_gla",
    )(pf, pj, pf, pf, wgk, bgk, ng)


def _chunk_attn_kernel(q_ref, k_ref, v_ref, bias_ref, o_ref, kpad_ref, vpad_ref):
    i = pl.program_id(1)
    s = k_ref.shape[0]

    @pl.when(i == 0)
    def _():
        zeros = jnp.zeros((B_PAD, BRANCH_WIDTH), BF16)
        kpad_ref[0:B_PAD, :] = zeros
        vpad_ref[0:B_PAD, :] = zeros
        kpad_ref[B_PAD:B_PAD + s, :] = k_ref[...]
        vpad_ref[B_PAD:B_PAD + s, :] = v_ref[...]

    start = pl.multiple_of(i * Q_BLOCK, Q_BLOCK)
    kw = kpad_ref[pl.ds(start, BAND), :]
    vw = vpad_ref[pl.ds(start, BAND), :]
    q = q_ref[...].astype(F32)
    in_seq = lax.broadcasted_iota(jnp.int32, (Q_BLOCK, BAND), 1) >= B_PAD - i * Q_BLOCK
    lane = lax.broadcasted_iota(jnp.int32, (Q_BLOCK, LANES), 1)

    for pair in range(B_HEADS // 2):
        cols = slice(pair * LANES, (pair + 1) * LANES)
        out = jnp.zeros((Q_BLOCK, LANES), F32)
        for sub in range(2):
            h = 2 * pair + sub
            mine = (lane // HEAD_DIM) == sub
            q_h = jnp.where(mine, q[:, cols], 0.0).astype(BF16)
            sc = _dot_nt(q_h, kw[:, cols]) * (HEAD_DIM ** -0.5) + bias_ref[h]
            sc = jnp.where(in_seq, sc, MASK_VALUE)
            m = jnp.max(sc, axis=-1, keepdims=True)
            p = jnp.exp(sc - m)
            p = p * (1.0 / jnp.sum(p, axis=-1, keepdims=True))
            o = _dot(p.astype(BF16), vw[:, cols])
            out = jnp.where(mine, o, out)
        o_ref[:, cols] = out.astype(o_ref.dtype)


def _chunk_attn(pj, bias, layer):
    b, s, _ = pj.shape
    w = BRANCH_WIDTH
    return pl.pallas_call(
        _chunk_attn_kernel,
        grid=(b, s // Q_BLOCK),
        in_specs=[
            pl.BlockSpec((None, Q_BLOCK, w), lambda bi, i: (bi, i, QKV_B_Q)),
            pl.BlockSpec((None, s, w), lambda bi, i: (bi, 0, QKV_B_K)),
            pl.BlockSpec((None, s, w), lambda bi, i: (bi, 0, QKV_B_V)),
            pl.BlockSpec((None, B_HEADS, Q_BLOCK, BAND), lambda bi, i: (layer, 0, 0, 0)),
        ],
        out_specs=pl.BlockSpec((None, Q_BLOCK, w), lambda bi, i: (bi, i, 0)),
        out_shape=jax.ShapeDtypeStruct((b, s, w), BF16),
        scratch_shapes=[pltpu.VMEM((s + B_PAD, w), BF16), pltpu.VMEM((s + B_PAD, w), BF16)],
        compiler_params=_params(
            ("parallel", "arbitrary"),
            2 * Q_BLOCK * w * 2 + 2 * s * w * 2 + B_HEADS * Q_BLOCK * BAND * 4,
            2 * (s + B_PAD) * w * 2 + 8 * 1024 * 1024),
        name="mixer_b_chunk_attn",
    )(pj, pj, pj, bias)


def _band_bias(rel_table):
    qi = jnp.arange(Q_BLOCK)[:, None]
    kj = jnp.arange(BAND)[None, :]
    rel = jnp.clip(B_PAD + qi - kj, -B_MAX_REL, B_MAX_REL) + B_MAX_REL
    in_band = (kj // CHUNK >= qi // CHUNK) & (kj // CHUNK <= qi // CHUNK + B_PREV_CHUNKS)
    return jnp.where(in_band[None, None], rel_table.astype(F32)[:, :, rel], MASK_VALUE)


def _rglru_kernel(g_ref, x_ref, cw_ref, cb_ref, wa_ref, ba_ref, wx_ref, bx_ref, lam_ref, o_ref,
                  xe_ref, a_ref, b_ref, h_ref, carry_ref):
    ts = x_ref.shape[0]
    halo = SUBLANES

    @pl.when(pl.program_id(1) == 0)
    def _():
        xe_ref[0:halo, :] = jnp.zeros((halo, BRANCH_WIDTH), F32)
        carry_ref[...] = jnp.zeros_like(carry_ref)

    x = x_ref[...]
    xe_ref[halo:halo + ts, :] = x
    xc = cb_ref[...]
    for j in range(C_CONV - 1):
        lag = C_CONV - 1 - j
        xc = xc + xe_ref[halo - lag:halo - lag + ts, :] * cw_ref[j:j + 1, :]
    xc = xc + x * cw_ref[C_CONV - 1:C_CONV, :]
    xe_ref[0:halo, :] = xe_ref[ts:ts + halo, :]

    xcb = xc.astype(BF16)
    r = _sigmoid(_dot(xcb, wa_ref[...]) + ba_ref[...])
    gate_i = _sigmoid(_dot(xcb, wx_ref[...]) + bx_ref[...])
    lam = lam_ref[...]
    softplus_neg_lam = jnp.maximum(-lam, 0.0) + _log1p_exp_neg_abs(lam)
    log_a = -C_POW * r * softplus_neg_lam
    a = jnp.exp(log_a)
    a_ref[...] = a
    b_ref[...] = jnp.sqrt(-jnp.tanh(log_a) * (a * a + 1.0)) * (gate_i * xc)

    sub = lax.broadcasted_iota(jnp.int32, (SUBLANES, BRANCH_WIDTH), 0)

    def group(gi, carry):
        rows = pl.ds(pl.multiple_of(gi * SUBLANES, SUBLANES), SUBLANES)
        a = a_ref[rows, :]
        bb = b_ref[rows, :]
        for d in (1, 2, 4):
            a_prev = pltpu.roll(a, d, axis=0)
            b_prev = pltpu.roll(bb, d, axis=0)
            has_prev = sub >= d
            bb = jnp.where(has_prev, a * b_prev + bb, bb)
            a = jnp.where(has_prev, a * a_prev, a)
        h = a * carry + bb
        h_ref[rows, :] = h
        return jnp.broadcast_to(h[SUBLANES - 1:SUBLANES, :], (SUBLANES, BRANCH_WIDTH))

    carry_ref[...] = lax.fori_loop(0, ts // SUBLANES, group, carry_ref[...])
    o_ref[...] = (jax.nn.gelu(g_ref[...]) * h_ref[...]).astype(o_ref.dtype)


def _rglru(pf, cw, cb, wa, ba, wx, bx, lam, layer, ts):
    b, s, _ = pf.shape
    w = BRANCH_WIDTH
    vec = pl.BlockSpec((None, 1, w), lambda bi, i: (layer, 0, 0))
    mat = pl.BlockSpec((None, w, w), lambda bi, i: (layer, 0, 0))
    return pl.pallas_call(
        _rglru_kernel,
        grid=(b, s // ts),
        in_specs=[
            pl.BlockSpec((None, ts, w), lambda bi, i: (bi, i, F_C_G)),
            pl.BlockSpec((None, ts, w), lambda bi, i: (bi, i, F_C_X)),
            pl.BlockSpec((None, C_CONV, w), lambda bi, i: (layer, 0, 0)),
            vec, mat, vec, mat, vec, vec,
        ],
        out_specs=pl.BlockSpec((None, ts, w), lambda bi, i: (bi, i, 0)),
        out_shape=jax.ShapeDtypeStruct((b, s, w), BF16),
        scratch_shapes=[
            pltpu.VMEM((ts + 2 * SUBLANES, w), F32),
            pltpu.VMEM((ts, w), F32),
            pltpu.VMEM((ts, w), F32),
            pltpu.VMEM((ts, w), F32),
            pltpu.VMEM((SUBLANES, w), F32),
        ],
        compiler_params=_params(("parallel", "arbitrary"), ts * w * 10 + 2 * w * w * 2,
                                4 * ts * w * 4 + 8 * 1024 * 1024),
        name="mixer_c_rglru",
    )(pf, pf, cw, cb, wa, ba, wx, bx, lam)


def _block_diag(wb):
    l, g, n, _ = wb.shape
    eye = jnp.eye(g, dtype=wb.dtype)
    return (wb[:, :, :, None, :] * eye[None, :, None, :, None]).reshape(l, g * n, g * n)


def _stick_kernel(q_ref, k_ref, v_ref, o_ref, qh_ref, later_ref, acc_ref):
    i = pl.program_id(1)
    q = q_ref[...].astype(F32)
    lane = lax.broadcasted_iota(jnp.int32, (Q_BLOCK, LANES), 1)
    row = lax.broadcasted_iota(jnp.int32, (Q_BLOCK, Q_BLOCK), 0)
    col = lax.broadcasted_iota(jnp.int32, (Q_BLOCK, Q_BLOCK), 1)
    strictly_before = col < row
    rr = lax.broadcasted_iota(jnp.int32, (Q_BLOCK, 2 * Q_BLOCK), 0)
    cc = lax.broadcasted_iota(jnp.int32, (Q_BLOCK, 2 * Q_BLOCK), 1)
    suffix_and_total = jnp.where((rr > cc) | (cc >= Q_BLOCK), 1.0, 0.0).astype(BF16)

    for h in range(D_HEADS):
        cols = slice((h // 2) * LANES, (h // 2 + 1) * LANES)
        qh_ref[h] = jnp.where((lane // HEAD_DIM) == (h % 2), q[:, cols], 0.0).astype(BF16)

    def key_block(j, diagonal):
        ks = pl.multiple_of(j * Q_BLOCK, Q_BLOCK)
        for pair in range(D_HEADS // 2):
            cols = slice(pair * LANES, (pair + 1) * LANES)
            k_pair = k_ref[pl.ds(ks, Q_BLOCK), cols]
            v_pair = v_ref[pl.ds(ks, Q_BLOCK), cols]
            for h in (2 * pair, 2 * pair + 1):
                z = _dot_nt(qh_ref[h], k_pair) * (HEAD_DIM ** -0.5)
                soft = _log1p_exp_neg_abs(z)
                log1m = -jnp.maximum(z, 0.0) - soft
                if diagonal:
                    log1m = jnp.where(strictly_before, log1m, 0.0)
                hi, lo = _two_bf16_terms(log1m)
                sums = _dot(hi, suffix_and_total) + _dot(lo, suffix_and_total)
                tail = sums[:, :Q_BLOCK]
                if not diagonal:
                    tail = tail + later_ref[h]
                att = jnp.exp(jnp.minimum(z, 0.0) - soft + tail)
                if diagonal:
                    att = jnp.where(strictly_before, att, 0.0)
                pv = _dot(att.astype(BF16), v_pair)
                if diagonal:
                    later_ref[h] = sums[:, Q_BLOCK:]
                    acc_ref[h] = pv
                else:
                    later_ref[h] = later_ref[h] + sums[:, Q_BLOCK:]
                    acc_ref[h] = acc_ref[h] + pv

    key_block(i, True)

    def earlier(jj, carry):
        key_block(i - 1 - jj, False)
        return carry

    lax.fori_loop(0, i, earlier, 0)

    for pair in range(D_HEADS // 2):
        out = jnp.where(lane < HEAD_DIM, acc_ref[2 * pair], acc_ref[2 * pair + 1])
        o_ref[:, pair * LANES:(pair + 1) * LANES] = out.astype(o_ref.dtype)


def _stick(pj):
    b, s, _ = pj.shape
    w = BRANCH_WIDTH
    return pl.pallas_call(
        _stick_kernel,
        grid=(b, s // Q_BLOCK),
        in_specs=[
            pl.BlockSpec((None, Q_BLOCK, w), lambda bi, i: (bi, i, QKV_D_Q)),
            pl.BlockSpec((None, s, w), lambda bi, i: (bi, 0, QKV_D_K)),
            pl.BlockSpec((None, s, w), lambda bi, i: (bi, 0, QKV_D_V)),
        ],
        out_specs=pl.BlockSpec((None, Q_BLOCK, w), lambda bi, i: (bi, i, 0)),
        out_shape=jax.ShapeDtypeStruct((b, s, w), BF16),
        scratch_shapes=[
            pltpu.VMEM((D_HEADS, Q_BLOCK, LANES), BF16),
            pltpu.VMEM((D_HEADS, Q_BLOCK, LANES), F32),
            pltpu.VMEM((D_HEADS, Q_BLOCK, LANES), F32),
        ],
        compiler_params=_params(("parallel", "arbitrary"), 2 * Q_BLOCK * w * 2 + 2 * s * w * 2,
                                8 * 1024 * 1024),
        name="mixer_d_stick",
    )(pj, pj, pj)


def _merge_kernel(ya_ref, yb_ref, yc_ref, yd_ref, gl_ref, h_ref, wb_ref, wo_ref, o_ref):
    d = h_ref.shape[1]
    mixed = None
    for n, y_ref in enumerate((ya_ref, yb_ref, yc_ref, yd_ref)):
        term = _sigmoid(gl_ref[:, n * d:(n + 1) * d]) * _dot(y_ref[...], wb_ref[n])
        mixed = term if mixed is None else mixed + term
    o_ref[...] = h_ref[...] + _dot(mixed.astype(BF16), wo_ref[...])


def _merge(ya, yb, yc, yd, pf, h, wb, wo, layer, tm):
    t, d = h.shape
    w = BRANCH_WIDTH
    y_spec = pl.BlockSpec((tm, w), lambda i: (i, 0))
    return pl.pallas_call(
        _merge_kernel,
        grid=(t // tm,),
        in_specs=[
            y_spec, y_spec, y_spec, y_spec,
            pl.BlockSpec((tm, N_BRANCH * d), lambda i: (i, 0)),
            pl.BlockSpec((tm, d), lambda i: (i, 0)),
            pl.BlockSpec((None, N_BRANCH, w, d), lambda i: (layer, 0, 0, 0)),
            pl.BlockSpec((None, d, d), lambda i: (layer, 0, 0)),
        ],
        out_specs=pl.BlockSpec((tm, d), lambda i: (i, 0)),
        out_shape=jax.ShapeDtypeStruct((t, d), F32),
        compiler_params=_params(
            ("parallel",),
            4 * tm * w * 2 + tm * N_BRANCH * d * 4 + 2 * tm * d * 4 + (N_BRANCH * w * d + d * d) * 2,
            4 * tm * d * 4),
        name="merge",
    )(ya, yb, yc, yd, pf, h, wb, wo)


def _ffn_kernel(h_ref, g_ref, wg_ref, wu_ref, wd_ref, gf_ref, o_ref, *, hidden_chunk, final_norm):
    h = h_ref[...]
    hn = (h * lax.rsqrt(jnp.mean(h * h, axis=-1, keepdims=True) + EPS) * g_ref[...]).astype(BF16)
    hidden = wg_ref.shape[1]
    acc = h
    for c0 in range(0, hidden, hidden_chunk):
        cs = slice(c0, min(c0 + hidden_chunk, hidden))
        gate = _dot(hn, wg_ref[:, cs])
        up = _dot(hn, wu_ref[:, cs])
        act = (gate * _sigmoid(gate) * up).astype(BF16)
        acc = acc + _dot(act, wd_ref[cs, :])
    if final_norm:
        acc = acc * lax.rsqrt(jnp.mean(acc * acc, axis=-1, keepdims=True) + EPS) * gf_ref[...]
    o_ref[...] = acc


def _ffn(h, g, wg, wu, wd, gf, layer, tm, final_norm):
    t, d = h.shape
    hidden = wg.shape[-1]
    return pl.pallas_call(
        functools.partial(_ffn_kernel, hidden_chunk=512, final_norm=final_norm),
        grid=(t // tm,),
        in_specs=[
            pl.BlockSpec((tm, d), lambda i: (i, 0)),
            pl.BlockSpec((None, 1, d), lambda i: (layer, 0, 0)),
            pl.BlockSpec((None, d, hidden), lambda i: (layer, 0, 0)),
            pl.BlockSpec((None, d, hidden), lambda i: (layer, 0, 0)),
            pl.BlockSpec((None, hidden, d), lambda i: (layer, 0, 0)),
            pl.BlockSpec((1, d), lambda i: (0, 0)),
        ],
        out_specs=pl.BlockSpec((tm, d), lambda i: (i, 0)),
        out_shape=jax.ShapeDtypeStruct((t, d), F32),
        compiler_params=_params(("parallel",), 2 * tm * d * 4 + 3 * d * hidden * 2,
                                3 * tm * d * 4 + 3 * tm * 512 * 4),
        name="ffn",
    )(h, g, wg, wu, wd, gf)


def kernel(x, norm_mix, w_in, a_w_gk, a_b_gk, a_norm, b_rel_bias, c_conv_w, c_conv_b, c_w_a, c_b_a,
           c_w_x, c_b_x, c_lambda, w_branch, w_out, norm_ffn, w_ffn_gate, w_ffn_up, w_ffn_down,
           norm_final):
    bsz, s, d = x.shape
    depth = w_in.shape[0]
    t = bsz * s
    w = BRANCH_WIDTH

    sizes = (A_HEADS * HEAD_DIM, A_HEADS * HEAD_DIM, A_HEADS * A_DV, A_RANK, A_HEADS * A_DV,
             w, w, w, w, w, w, w, w, N_BRANCH * d)
    offs = [0]
    for sz in sizes:
        offs.append(offs[-1] + sz)
    (o_aq, o_ak, o_av, o_ar, o_ag, o_bq, _, _, o_cg, _, o_dq, _, _, o_gate, o_end) = offs

    w_qkv = jnp.concatenate(
        [w_in[:, :, o_av:o_ar], w_in[:, :, o_bq:o_cg], w_in[:, :, o_dq:o_gate]], axis=-1).astype(BF16)
    w_f = jnp.concatenate(
        [w_in[:, :, o_gate:o_end], w_in[:, :, o_ag:o_bq], w_in[:, :, o_cg:o_dq],
         w_in[:, :, o_aq:o_av], w_in[:, :, o_ar:o_ag],
         jnp.zeros((depth, d, 2 * LANES - A_RANK), w_in.dtype)], axis=-1).astype(BF16)
    wgk = jnp.pad(a_w_gk, ((0, 0), (0, LANES - A_RANK), (0, 0))).astype(BF16)
    vec = lambda p: p.reshape(depth, 1, -1)
    bias = _band_bias(b_rel_bias)
    wa = _block_diag(c_w_a).astype(BF16)
    wx = _block_diag(c_w_x).astype(BF16)
    wb = w_branch.astype(BF16)
    wo = w_out.astype(BF16)
    wg = w_ffn_gate.astype(BF16)
    wu = w_ffn_up.astype(BF16)
    wd = w_ffn_down.astype(BF16)
    gf = norm_final.reshape(1, d)

    h = x.reshape(t, d)
    for layer in range(depth):
        pj = _norm_matmul(h, vec(norm_mix), w_qkv, layer, BF16, 1024, 1792).reshape(bsz, s, QKV_COLS)
        pf = _norm_matmul(h, vec(norm_mix), w_f, layer, F32, 1024, 1280).reshape(bsz, s, F_COLS)
        ya = _gla(pf, pj, wgk, vec(a_b_gk), vec(a_norm), layer, 512)
        yb = _chunk_attn(pj, bias, layer)
        yc = _rglru(pf, c_conv_w, vec(c_conv_b), wa, vec(c_b_a), wx, vec(c_b_x), vec(c_lambda),
                    layer, 512)
        yd = _stick(pj)
        h = _merge(ya.reshape(t, w), yb.reshape(t, w), yc.reshape(t, w), yd.reshape(t, w),
                   pf.reshape(t, F_COLS), h, wb, wo, layer, 512)
        h = _ffn(h, vec(norm_ffn), wg, wu, wd, gf, layer, 512, layer == depth - 1)
    return h.reshape(bsz, s, d)
```

```python
import functools

import jax
import jax.numpy as jnp
from jax import lax
from jax.experimental import pallas as pl
from jax.experimental.pallas import tpu as pltpu

F32 = jnp.float32
BF16 = jnp.bfloat16

EPS = 1e-6
CHUNK = 64
BRANCH_WIDTH = 512
N_BRANCH = 4
HEAD_DIM = 64
A_HEADS = 4
A_DV = 128
A_RANK = 16
A_GATE_NORM = 16.0
B_HEADS = 8
B_PREV_CHUNKS = 8
B_MAX_REL = 128
C_BLOCKS = 8
C_CONV = 4
C_POW = 8.0
D_HEADS = 8
MASK_VALUE = -1e30
EXP_UNDERFLOW = 104.0

LANES = 128
SUBLANES = 8
VMEM_BYTES_V7X = 64 * 1024 * 1024

QKV_COLS = 7 * BRANCH_WIDTH
QKV_A_V, QKV_B_Q, QKV_B_K, QKV_B_V, QKV_D_Q, QKV_D_K, QKV_D_V = range(7)
GATE_COLS = N_BRANCH * 2 * BRANCH_WIDTH
F_A_G, F_C_G, F_C_X, F_A_QK = 8, 9, 10, 11
F_A_R_COL = 12 * BRANCH_WIDTH
F_COLS = F_A_R_COL + 2 * LANES

Q_BLOCK = 128
BAND = (B_PREV_CHUNKS + 2) * CHUNK
B_PAD = B_PREV_CHUNKS * CHUNK


def _dot(a, b):
    return jnp.dot(a, b, preferred_element_type=F32)


def _dot_nt(a, b):
    return lax.dot_general(a, b, (((1,), (1,)), ((), ())), preferred_element_type=F32)


def _dot_tn(a, b):
    return lax.dot_general(a, b, (((0,), (0,)), ((), ())), preferred_element_type=F32)


def _two_bf16_terms(x):
    hi = x.astype(BF16)
    lo = (x - hi.astype(F32)).astype(BF16)
    return hi, lo


def _log1p_exp_neg_abs(z):
    return jnp.log1p(jnp.exp(-jnp.abs(z)))


def _log_sigmoid(z):
    return jnp.minimum(z, 0.0) - _log1p_exp_neg_abs(z)


def _sigmoid(z):
    return 1.0 / (1.0 + jnp.exp(-z))


def _vmem_limit(pipelined_bytes, resident_bytes):
    want = 2 * pipelined_bytes + resident_bytes
    return int(min(max(want, 16 * 1024 * 1024), VMEM_BYTES_V7X - 8 * 1024 * 1024))


def _params(semantics, pipelined_bytes, resident_bytes):
    return pltpu.CompilerParams(
        dimension_semantics=semantics,
        vmem_limit_bytes=_vmem_limit(pipelined_bytes, resident_bytes))


def _norm_matmul_kernel(x_ref, g_ref, w_ref, o_ref, xn_ref):
    @pl.when(pl.program_id(1) == 0)
    def _():
        x = x_ref[...]
        inv = lax.rsqrt(jnp.mean(x * x, axis=-1, keepdims=True) + EPS)
        xn_ref[...] = (x * inv * g_ref[...]).astype(BF16)

    o_ref[...] = _dot(xn_ref[...], w_ref[...]).astype(o_ref.dtype)


def _norm_matmul(x, g, w, layer, out_dtype, tm, tn):
    t, d = x.shape
    n = w.shape[-1]
    out_bytes = jnp.dtype(out_dtype).itemsize
    return pl.pallas_call(
        _norm_matmul_kernel,
        grid=(t // tm, n // tn),
        in_specs=[
            pl.BlockSpec((tm, d), lambda i, j: (i, 0)),
            pl.BlockSpec((None, 1, d), lambda i, j: (layer, 0, 0)),
            pl.BlockSpec((None, d, tn), lambda i, j: (layer, 0, j)),
        ],
        out_specs=pl.BlockSpec((tm, tn), lambda i, j: (i, j)),
        out_shape=jax.ShapeDtypeStruct((t, n), out_dtype),
        scratch_shapes=[pltpu.VMEM((tm, d), BF16)],
        compiler_params=_params(
            ("parallel", "arbitrary"),
            tm * d * 4 + d * tn * 2 + tm * tn * out_bytes,
            tm * d * 2 + tm * d * 4 + tm * tn * 4),
        name="norm_proj",
    )(x, g, w)


def _gla_kernel(qk_ref, v_ref, g_ref, r_ref, wgk_ref, bgk_ref, ng_ref, o_ref, state_ref, *,
                n_chunks):
    dk_all = A_HEADS * HEAD_DIM

    @pl.when(pl.program_id(1) == 0)
    def _():
        state_ref[...] = jnp.zeros_like(state_ref)

    row = lax.broadcasted_iota(jnp.int32, (CHUNK, CHUNK), 0)
    col = lax.broadcasted_iota(jnp.int32, (CHUNK, CHUNK), 1)
    tri = jnp.where(row >= col, 1.0, 0.0).astype(BF16)
    lane_head = lax.broadcasted_iota(jnp.int32, (CHUNK, dk_all), 1) // HEAD_DIM

    for c in range(n_chunks):
        rows = slice(c * CHUNK, (c + 1) * CHUNK)
        r = r_ref[rows, :].astype(BF16)
        gk = _log_sigmoid(_dot(r, wgk_ref[...]) + bgk_ref[...]) * (1.0 / A_GATE_NORM)
        gk_hi, gk_lo = _two_bf16_terms(gk)
        cum = _dot(tri, gk_hi) + _dot(tri, gk_lo)
        tot = cum[CHUNK - 1:CHUNK, :]
        qk = qk_ref[rows, :]
        k_dec = (qk[:, dk_all:] * jnp.exp(tot - cum)).astype(BF16)
        q = qk[:, :dk_all] * (HEAD_DIM ** -0.5)
        kv = _dot_tn(v_ref[rows, :], k_dec)
        state = state_ref[...] * jnp.exp(tot) + kv
        state_ref[...] = state
        state_b = state.astype(BF16)
        for h in range(A_HEADS):
            cols = slice(h * A_DV, (h + 1) * A_DV)
            q_h = jnp.where(lane_head == h, q, 0.0).astype(BF16)
            o = _dot_nt(q_h, state_b[cols, :])
            o = o * lax.rsqrt(jnp.mean(o * o, axis=-1, keepdims=True) + EPS) * ng_ref[...]
            g = g_ref[rows, cols]
            o_ref[rows, cols] = (o * (g * _sigmoid(g))).astype(o_ref.dtype)


def _gla(pf, pj, wgk, bgk, ng, layer, tq):
    b, s, _ = pf.shape
    w = BRANCH_WIDTH
    return pl.pallas_call(
        functools.partial(_gla_kernel, n_chunks=tq // CHUNK),
        grid=(b, s // tq),
        in_specs=[
            pl.BlockSpec((None, tq, w), lambda bi, i: (bi, i, F_A_QK)),
            pl.BlockSpec((None, tq, w), lambda bi, i: (bi, i, QKV_A_V)),
            pl.BlockSpec((None, tq, w), lambda bi, i: (bi, i, F_A_G)),
            pl.BlockSpec((None, tq, LANES), lambda bi, i: (bi, i, F_A_R_COL // LANES)),
            pl.BlockSpec((None, LANES, A_HEADS * HEAD_DIM), lambda bi, i: (layer, 0, 0)),
            pl.BlockSpec((None, 1, A_HEADS * HEAD_DIM), lambda bi, i: (layer, 0, 0)),
            pl.BlockSpec((None, 1, A_DV), lambda bi, i: (layer, 0, 0)),
        ],
        out_specs=pl.BlockSpec((None, tq, w), lambda bi, i: (bi, i, 0)),
        out_shape=jax.ShapeDtypeStruct((b, s, w), BF16),
        scratch_shapes=[pltpu.VMEM((A_HEADS * A_DV, A_HEADS * HEAD_DIM), F32)],
        compiler_params=_params(("parallel", "arbitrary"), tq * w * 12 + tq * LANES * 4,
                                8 * 1024 * 1024),
        name="mixer_a_gla",
    )(pf, pj, pf, pf, wgk, bgk, ng)


def _chunk_attn_kernel(q_ref, k_ref, v_ref, bias_ref, o_ref, qs_ref, kpad_ref, vfirst_ref,
                       vsecond_ref):
    i = pl.program_id(1)
    s = k_ref.shape[0]
    n_pairs = B_HEADS // 2
    init_rows = 2 * Q_BLOCK

    @pl.when(i == 0)
    def _():
        zeros = jnp.zeros((B_PAD, BRANCH_WIDTH), BF16)
        kpad_ref[0:B_PAD, :] = zeros
        vfirst_ref[0:B_PAD, :] = zeros
        vsecond_ref[0:B_PAD, :] = zeros
        kpad_ref[B_PAD:B_PAD + s, :] = k_ref[...]
        lane = lax.broadcasted_iota(jnp.int32, (init_rows, BRANCH_WIDTH), 1)
        first = (lane % LANES) < HEAD_DIM
        for r0 in range(0, s, init_rows):
            v = v_ref[r0:r0 + init_rows, :].astype(F32)
            vfirst_ref[B_PAD + r0:B_PAD + r0 + init_rows, :] = jnp.where(first, v, 0.0).astype(BF16)
            vsecond_ref[B_PAD + r0:B_PAD + r0 + init_rows, :] = jnp.where(first, 0.0, v).astype(BF16)

    first_head = lax.broadcasted_iota(jnp.int32, (Q_BLOCK, LANES), 1) < HEAD_DIM
    q = q_ref[...].astype(F32) * (HEAD_DIM ** -0.5)
    for p in range(n_pairs):
        q_pair = q[:, p * LANES:(p + 1) * LANES]
        qs_ref[p, 0:Q_BLOCK, :] = jnp.where(first_head, q_pair, 0.0).astype(BF16)
        qs_ref[p, Q_BLOCK:2 * Q_BLOCK, :] = jnp.where(first_head, 0.0, q_pair).astype(BF16)

    start = pl.multiple_of(i * Q_BLOCK, Q_BLOCK)
    window = pl.ds(start, BAND)

    def attend(has_padding):
        pairs = range(n_pairs)
        cols = [slice(p * LANES, (p + 1) * LANES) for p in pairs]
        sc = [_dot_nt(qs_ref[p], kpad_ref[window, cols[p]]) + bias_ref[p] for p in pairs]
        if has_padding:
            in_seq = lax.broadcasted_iota(jnp.int32, (2 * Q_BLOCK, BAND), 1) >= B_PAD - i * Q_BLOCK
            sc = [jnp.where(in_seq, x, MASK_VALUE) for x in sc]
        e = [jnp.exp(x - jnp.max(x, axis=-1, keepdims=True)) for x in sc]
        prob = [(x * (1.0 / jnp.sum(x, axis=-1, keepdims=True))).astype(BF16) for x in e]
        for p in pairs:
            both = jnp.concatenate([prob[p][:Q_BLOCK], prob[p][Q_BLOCK:]], axis=1)
            values = jnp.concatenate([vfirst_ref[window, cols[p]], vsecond_ref[window, cols[p]]], axis=0)
            o_ref[:, cols[p]] = _dot(both, values).astype(o_ref.dtype)

    @pl.when(i < B_PAD // Q_BLOCK)
    def _():
        attend(True)

    @pl.when(i >= B_PAD // Q_BLOCK)
    def _():
        attend(False)


def _chunk_attn(pj, bias, layer):
    b, s, _ = pj.shape
    w = BRANCH_WIDTH
    return pl.pallas_call(
        _chunk_attn_kernel,
        grid=(b, s // Q_BLOCK),
        in_specs=[
            pl.BlockSpec((None, Q_BLOCK, w), lambda bi, i: (bi, i, QKV_B_Q)),
            pl.BlockSpec((None, s, w), lambda bi, i: (bi, 0, QKV_B_K)),
            pl.BlockSpec((None, s, w), lambda bi, i: (bi, 0, QKV_B_V)),
            pl.BlockSpec((None, B_HEADS // 2, 2 * Q_BLOCK, BAND), lambda bi, i: (layer, 0, 0, 0)),
        ],
        out_specs=pl.BlockSpec((None, Q_BLOCK, w), lambda bi, i: (bi, i, 0)),
        out_shape=jax.ShapeDtypeStruct((b, s, w), BF16),
        scratch_shapes=[
            pltpu.VMEM((B_HEADS // 2, 2 * Q_BLOCK, LANES), BF16),
            pltpu.VMEM((s + B_PAD, w), BF16),
            pltpu.VMEM((s + B_PAD, w), BF16),
            pltpu.VMEM((s + B_PAD, w), BF16),
        ],
        compiler_params=_params(
            ("parallel", "arbitrary"),
            2 * Q_BLOCK * w * 2 + 2 * s * w * 2 + B_HEADS * Q_BLOCK * BAND * 4,
            3 * (s + B_PAD) * w * 2 + 16 * 1024 * 1024),
        name="mixer_b_chunk_attn",
    )(pj, pj, pj, bias)


def _band_bias(rel_table):
    depth, heads, _ = rel_table.shape
    width = BAND + Q_BLOCK
    ramp_idx = jnp.clip((BAND - 1) - jnp.arange(width), -B_MAX_REL, B_MAX_REL) + B_MAX_REL
    ramp = rel_table.astype(F32)[:, :, ramp_idx]
    flat = jnp.tile(ramp, (1, 1, Q_BLOCK))[:, :, :Q_BLOCK * (width - 1)]
    skew = flat.reshape(depth, heads, Q_BLOCK, width - 1)
    bias = skew[:, :, :, Q_BLOCK - 1:Q_BLOCK - 1 + BAND]
    qi = jnp.arange(Q_BLOCK)[:, None]
    kj = jnp.arange(BAND)[None, :]
    in_band = (kj // CHUNK >= qi // CHUNK) & (kj // CHUNK <= qi // CHUNK + B_PREV_CHUNKS)
    bias = jnp.where(in_band[None, None], bias, MASK_VALUE)
    return bias.reshape(depth, heads // 2, 2 * Q_BLOCK, BAND)


def _rglru_kernel(g_ref, x_ref, cw_ref, cb_ref, wa_ref, ba_ref, wx_ref, bx_ref, lam_ref, o_ref,
                  xe_ref, a_ref, b_ref, h_ref, carry_ref):
    ts = x_ref.shape[0]
    halo = SUBLANES

    @pl.when(pl.program_id(1) == 0)
    def _():
        xe_ref[0:halo, :] = jnp.zeros((halo, BRANCH_WIDTH), F32)
        carry_ref[...] = jnp.zeros_like(carry_ref)

    x = x_ref[...]
    xe_ref[halo:halo + ts, :] = x
    xc = cb_ref[...]
    for j in range(C_CONV - 1):
        lag = C_CONV - 1 - j
        xc = xc + xe_ref[halo - lag:halo - lag + ts, :] * cw_ref[j:j + 1, :]
    xc = xc + x * cw_ref[C_CONV - 1:C_CONV, :]
    xe_ref[0:halo, :] = xe_ref[ts:ts + halo, :]

    xcb = xc.astype(BF16)
    r = _sigmoid(_dot(xcb, wa_ref[...]) + ba_ref[...])
    gate_i = _sigmoid(_dot(xcb, wx_ref[...]) + bx_ref[...])
    lam = lam_ref[...]
    softplus_neg_lam = jnp.maximum(-lam, 0.0) + _log1p_exp_neg_abs(lam)
    log_a = -C_POW * r * softplus_neg_lam
    a = jnp.exp(log_a)
    a_ref[...] = a
    b_ref[...] = jnp.sqrt(-jnp.tanh(log_a) * (a * a + 1.0)) * (gate_i * xc)

    sub = lax.broadcasted_iota(jnp.int32, (SUBLANES, BRANCH_WIDTH), 0)

    def group(gi, carry):
        rows = pl.ds(pl.multiple_of(gi * SUBLANES, SUBLANES), SUBLANES)
        a = a_ref[rows, :]
        bb = b_ref[rows, :]
        for d in (1, 2, 4):
            a_prev = pltpu.roll(a, d, axis=0)
            b_prev = pltpu.roll(bb, d, axis=0)
            has_prev = sub >= d
            bb = jnp.where(has_prev, a * b_prev + bb, bb)
            a = jnp.where(has_prev, a * a_prev, a)
        h = a * carry + bb
        h_ref[rows, :] = h
        return jnp.broadcast_to(h[SUBLANES - 1:SUBLANES, :], (SUBLANES, BRANCH_WIDTH))

    carry_ref[...] = lax.fori_loop(0, ts // SUBLANES, group, carry_ref[...])
    o_ref[...] = (jax.nn.gelu(g_ref[...]) * h_ref[...]).astype(o_ref.dtype)


def _rglru(pf, cw, cb, wa, ba, wx, bx, lam, layer, ts):
    b, s, _ = pf.shape
    w = BRANCH_WIDTH
    vec = pl.BlockSpec((None, 1, w), lambda bi, i: (layer, 0, 0))
    mat = pl.BlockSpec((None, w, w), lambda bi, i: (layer, 0, 0))
    return pl.pallas_call(
        _rglru_kernel,
        grid=(b, s // ts),
        in_specs=[
            pl.BlockSpec((None, ts, w), lambda bi, i: (bi, i, F_C_G)),
            pl.BlockSpec((None, ts, w), lambda bi, i: (bi, i, F_C_X)),
            pl.BlockSpec((None, C_CONV, w), lambda bi, i: (layer, 0, 0)),
            vec, mat, vec, mat, vec, vec,
        ],
        out_specs=pl.BlockSpec((None, ts, w), lambda bi, i: (bi, i, 0)),
        out_shape=jax.ShapeDtypeStruct((b, s, w), BF16),
        scratch_shapes=[
            pltpu.VMEM((ts + 2 * SUBLANES, w), F32),
            pltpu.VMEM((ts, w), F32),
            pltpu.VMEM((ts, w), F32),
            pltpu.VMEM((ts, w), F32),
            pltpu.VMEM((SUBLANES, w), F32),
        ],
        compiler_params=_params(("parallel", "arbitrary"), ts * w * 10 + 2 * w * w * 2,
                                4 * ts * w * 4 + 8 * 1024 * 1024),
        name="mixer_c_rglru",
    )(pf, pf, cw, cb, wa, ba, wx, bx, lam)


def _block_diag(wb):
    l, g, n, _ = wb.shape
    eye = jnp.eye(g, dtype=wb.dtype)
    return (wb[:, :, :, None, :] * eye[None, :, None, :, None]).reshape(l, g * n, g * n)


def _stick_kernel(q_ref, k_ref, v_ref, o_ref, qs_ref, vs_ref, later_ref, acc_ref):
    i = pl.program_id(1)
    n_pairs = D_HEADS // 2
    n_key_blocks = k_ref.shape[0] // Q_BLOCK
    first_head = lax.broadcasted_iota(jnp.int32, (Q_BLOCK, LANES), 1) < HEAD_DIM

    @pl.when(i == 0)
    def _():
        for jb in range(n_key_blocks):
            for p in range(n_pairs):
                v = v_ref[jb * Q_BLOCK:(jb + 1) * Q_BLOCK, p * LANES:(p + 1) * LANES].astype(F32)
                vs_ref[jb, p, 0:Q_BLOCK, :] = jnp.where(first_head, v, 0.0).astype(BF16)
                vs_ref[jb, p, Q_BLOCK:2 * Q_BLOCK, :] = jnp.where(first_head, 0.0, v).astype(BF16)

    q = q_ref[...].astype(F32) * (HEAD_DIM ** -0.5)
    for p in range(n_pairs):
        q_pair = q[:, p * LANES:(p + 1) * LANES]
        qs_ref[p, 0:Q_BLOCK, :] = jnp.where(first_head, q_pair, 0.0).astype(BF16)
        qs_ref[p, Q_BLOCK:2 * Q_BLOCK, :] = jnp.where(first_head, 0.0, q_pair).astype(BF16)

    row = lax.broadcasted_iota(jnp.int32, (2 * Q_BLOCK, Q_BLOCK), 0) % Q_BLOCK
    col = lax.broadcasted_iota(jnp.int32, (2 * Q_BLOCK, Q_BLOCK), 1)
    strictly_before = col < row
    rr = lax.broadcasted_iota(jnp.int32, (2 * Q_BLOCK, 2 * Q_BLOCK), 0) % Q_BLOCK
    cc = lax.broadcasted_iota(jnp.int32, (2 * Q_BLOCK, 2 * Q_BLOCK), 1)
    suffix_and_total = jnp.where((rr > cc) | (cc >= Q_BLOCK), 1.0, 0.0).astype(BF16)

    def key_block(j, diagonal):
        ks = pl.multiple_of(j * Q_BLOCK, Q_BLOCK)
        pairs = range(n_pairs)
        z = [_dot_nt(qs_ref[p], k_ref[pl.ds(ks, Q_BLOCK), p * LANES:(p + 1) * LANES]) for p in pairs]
        neg_log1m, log_sig = [], []
        for p in pairs:
            t = jnp.maximum(z[p], 0.0) + jnp.log(1.0 + jnp.exp(-jnp.abs(z[p])))
            log_sig.append(z[p] - t)
            neg_log1m.append(jnp.where(strictly_before, t, 0.0) if diagonal else t)
        sums = []
        for p in pairs:
            hi, lo = _two_bf16_terms(neg_log1m[p])
            sums.append(_dot(jnp.concatenate([hi, lo], axis=1), suffix_and_total))
        att = []
        for p in pairs:
            tail = sums[p][:, :Q_BLOCK]
            total = sums[p][:, Q_BLOCK:]
            if not diagonal:
                tail = tail + later_ref[p]
                total = total + later_ref[p]
            later_ref[p] = total
            a = jnp.exp(log_sig[p] - tail)
            if diagonal:
                a = jnp.where(strictly_before, a, 0.0)
            a = a.astype(BF16)
            att.append(jnp.concatenate([a[:Q_BLOCK], a[Q_BLOCK:]], axis=1))
        for p in pairs:
            pv = _dot(att[p], vs_ref[j, p])
            acc_ref[p] = pv if diagonal else acc_ref[p] + pv

    def least_later():
        m = later_ref[0]
        for p in range(1, n_pairs):
            m = jnp.minimum(m, later_ref[p])
        return jnp.min(m)

    key_block(i, True)

    def more(carry):
        jj, least = carry
        return jnp.logical_and(jj < i, least < EXP_UNDERFLOW)

    def earlier(carry):
        jj, _ = carry
        key_block(i - 1 - jj, False)
        return jj + 1, least_later()

    lax.while_loop(more, earlier, (jnp.int32(0), least_later()))

    for p in range(n_pairs):
        o_ref[:, p * LANES:(p + 1) * LANES] = acc_ref[p].astype(o_ref.dtype)


def _stick(pj):
    b, s, _ = pj.shape
    w = BRANCH_WIDTH
    return pl.pallas_call(
        _stick_kernel,
        grid=(b, s // Q_BLOCK),
        in_specs=[
            pl.BlockSpec((None, Q_BLOCK, w), lambda bi, i: (bi, i, QKV_D_Q)),
            pl.BlockSpec((None, s, w), lambda bi, i: (bi, 0, QKV_D_K)),
            pl.BlockSpec((None, s, w), lambda bi, i: (bi, 0, QKV_D_V)),
        ],
        out_specs=pl.BlockSpec((None, Q_BLOCK, w), lambda bi, i: (bi, i, 0)),
        out_shape=jax.ShapeDtypeStruct((b, s, w), BF16),
        scratch_shapes=[
            pltpu.VMEM((D_HEADS // 2, 2 * Q_BLOCK, LANES), BF16),
            pltpu.VMEM((s // Q_BLOCK, D_HEADS // 2, 2 * Q_BLOCK, LANES), BF16),
            pltpu.VMEM((D_HEADS // 2, 2 * Q_BLOCK, LANES), F32),
            pltpu.VMEM((D_HEADS // 2, Q_BLOCK, LANES), F32),
        ],
        compiler_params=_params(("parallel", "arbitrary"), 2 * Q_BLOCK * w * 2 + 2 * s * w * 2,
                                2 * s * w * 2 + 8 * 1024 * 1024),
        name="mixer_d_stick",
    )(pj, pj, pj)


def _merge_kernel(ya_ref, yb_ref, yc_ref, yd_ref, gl_ref, h_ref, wb_ref, wo_ref, o_ref):
    d = h_ref.shape[1]
    mixed = None
    for n, y_ref in enumerate((ya_ref, yb_ref, yc_ref, yd_ref)):
        term = _sigmoid(gl_ref[:, n * d:(n + 1) * d]) * _dot(y_ref[...], wb_ref[n])
        mixed = term if mixed is None else mixed + term
    o_ref[...] = h_ref[...] + _dot(mixed.astype(BF16), wo_ref[...])


def _merge(ya, yb, yc, yd, pf, h, wb, wo, layer, tm):
    t, d = h.shape
    w = BRANCH_WIDTH
    y_spec = pl.BlockSpec((tm, w), lambda i: (i, 0))
    return pl.pallas_call(
        _merge_kernel,
        grid=(t // tm,),
        in_specs=[
            y_spec, y_spec, y_spec, y_spec,
            pl.BlockSpec((tm, N_BRANCH * d), lambda i: (i, 0)),
            pl.BlockSpec((tm, d), lambda i: (i, 0)),
            pl.BlockSpec((None, N_BRANCH, w, d), lambda i: (layer, 0, 0, 0)),
            pl.BlockSpec((None, d, d), lambda i: (layer, 0, 0)),
        ],
        out_specs=pl.BlockSpec((tm, d), lambda i: (i, 0)),
        out_shape=jax.ShapeDtypeStruct((t, d), F32),
        compiler_params=_params(
            ("parallel",),
            4 * tm * w * 2 + tm * N_BRANCH * d * 4 + 2 * tm * d * 4 + (N_BRANCH * w * d + d * d) * 2,
            4 * tm * d * 4),
        name="merge",
    )(ya, yb, yc, yd, pf, h, wb, wo)


def _ffn_kernel(h_ref, g_ref, wg_ref, wu_ref, wd_ref, gf_ref, o_ref, *, hidden_chunk, final_norm):
    h = h_ref[...]
    hn = (h * lax.rsqrt(jnp.mean(h * h, axis=-1, keepdims=True) + EPS) * g_ref[...]).astype(BF16)
    hidden = wg_ref.shape[1]
    acc = h
    for c0 in range(0, hidden, hidden_chunk):
        cs = slice(c0, min(c0 + hidden_chunk, hidden))
        gate = _dot(hn, wg_ref[:, cs])
        up = _dot(hn, wu_ref[:, cs])
        act = (gate * _sigmoid(gate) * up).astype(BF16)
        acc = acc + _dot(act, wd_ref[cs, :])
    if final_norm:
        acc = acc * lax.rsqrt(jnp.mean(acc * acc, axis=-1, keepdims=True) + EPS) * gf_ref[...]
    o_ref[...] = acc


def _ffn(h, g, wg, wu, wd, gf, layer, tm, final_norm):
    t, d = h.shape
    hidden = wg.shape[-1]
    return pl.pallas_call(
        functools.partial(_ffn_kernel, hidden_chunk=512, final_norm=final_norm),
        grid=(t // tm,),
        in_specs=[
            pl.BlockSpec((tm, d), lambda i: (i, 0)),
            pl.BlockSpec((None, 1, d), lambda i: (layer, 0, 0)),
            pl.BlockSpec((None, d, hidden), lambda i: (layer, 0, 0)),
            pl.BlockSpec((None, d, hidden), lambda i: (layer, 0, 0)),
            pl.BlockSpec((None, hidden, d), lambda i: (layer, 0, 0)),
            pl.BlockSpec((1, d), lambda i: (0, 0)),
        ],
        out_specs=pl.BlockSpec((tm, d), lambda i: (i, 0)),
        out_shape=jax.ShapeDtypeStruct((t, d), F32),
        compiler_params=_params(("parallel",), 2 * tm * d * 4 + 3 * d * hidden * 2,
                                3 * tm * d * 4 + 3 * tm * 512 * 4),
        name="ffn",
    )(h, g, wg, wu, wd, gf)


def kernel(x, norm_mix, w_in, a_w_gk, a_b_gk, a_norm, b_rel_bias, c_conv_w, c_conv_b, c_w_a, c_b_a,
           c_w_x, c_b_x, c_lambda, w_branch, w_out, norm_ffn, w_ffn_gate, w_ffn_up, w_ffn_down,
           norm_final):
    bsz, s, d = x.shape
    depth = w_in.shape[0]
    t = bsz * s
    w = BRANCH_WIDTH

    sizes = (A_HEADS * HEAD_DIM, A_HEADS * HEAD_DIM, A_HEADS * A_DV, A_RANK, A_HEADS * A_DV,
             w, w, w, w, w, w, w, w, N_BRANCH * d)
    offs = [0]
    for sz in sizes:
        offs.append(offs[-1] + sz)
    (o_aq, o_ak, o_av, o_ar, o_ag, o_bq, _, _, o_cg, _, o_dq, _, _, o_gate, o_end) = offs

    w_qkv = jnp.concatenate(
        [w_in[:, :, o_av:o_ar], w_in[:, :, o_bq:o_cg], w_in[:, :, o_dq:o_gate]], axis=-1).astype(BF16)
    w_f = jnp.concatenate(
        [w_in[:, :, o_gate:o_end], w_in[:, :, o_ag:o_bq], w_in[:, :, o_cg:o_dq],
         w_in[:, :, o_aq:o_av], w_in[:, :, o_ar:o_ag],
         jnp.zeros((depth, d, 2 * LANES - A_RANK), w_in.dtype)], axis=-1).astype(BF16)
    wgk = jnp.pad(a_w_gk, ((0, 0), (0, LANES - A_RANK), (0, 0))).astype(BF16)
    vec = lambda p: p.reshape(depth, 1, -1)
    bias = _band_bias(b_rel_bias)
    wa = _block_diag(c_w_a).astype(BF16)
    wx = _block_diag(c_w_x).astype(BF16)
    wb = w_branch.astype(BF16)
    wo = w_out.astype(BF16)
    wg = w_ffn_gate.astype(BF16)
    wu = w_ffn_up.astype(BF16)
    wd = w_ffn_down.astype(BF16)
    gf = norm_final.reshape(1, d)

    h = x.reshape(t, d)
    for layer in range(depth):
        pj = _norm_matmul(h, vec(norm_mix), w_qkv, layer, BF16, 1024, 1792).reshape(bsz, s, QKV_COLS)
        pf = _norm_matmul(h, vec(norm_mix), w_f, layer, F32, 1024, 1280).reshape(bsz, s, F_COLS)
        ya = _gla(pf, pj, wgk, vec(a_b_gk), vec(a_norm), layer, 512)
        yb = _chunk_attn(pj, bias, layer)
        yc = _rglru(pf, c_conv_w, vec(c_conv_b), wa, vec(c_b_a), wx, vec(c_b_x), vec(c_lambda),
                    layer, 512)
        yd = _stick(pj)
        h = _merge(ya.reshape(t, w), yb.reshape(t, w), yc.reshape(t, w), yd.reshape(t, w),
                   pf.reshape(t, F_COLS), h, wb, wo, layer, 512)
        h = _ffn(h, vec(norm_ffn), wg, wu, wd, gf, layer, 512, layer == depth - 1)
    return h.reshape(bsz, s, d)
```

```python
import functools

import jax
import jax.numpy as jnp
from jax import lax
from jax.experimental import pallas as pl
from jax.experimental.pallas import tpu as pltpu

F32 = jnp.float32
BF16 = jnp.bfloat16

EPS = 1e-6
CHUNK = 64
BRANCH_WIDTH = 512
N_BRANCH = 4
HEAD_DIM = 64
A_HEADS = 4
A_DV = 128
A_RANK = 16
A_GATE_NORM = 16.0
B_HEADS = 8
B_PREV_CHUNKS = 8
B_MAX_REL = 128
C_BLOCKS = 8
C_CONV = 4
C_POW = 8.0
D_HEADS = 8
MASK_VALUE = -1e30
EXP_UNDERFLOW = 104.0

LANES = 128
SUBLANES = 8
VMEM_BYTES_V7X = 64 * 1024 * 1024

QKV_COLS = 7 * BRANCH_WIDTH
QKV_A_V, QKV_B_Q, QKV_B_K, QKV_B_V, QKV_D_Q, QKV_D_K, QKV_D_V = range(7)
GATE_COLS = N_BRANCH * 2 * BRANCH_WIDTH
F_A_G, F_C_G, F_C_X, F_A_QK = 8, 9, 10, 11
F_A_R_COL = 12 * BRANCH_WIDTH
F_COLS = F_A_R_COL + 2 * LANES

Q_BLOCK = 128
BAND = (B_PREV_CHUNKS + 2) * CHUNK
B_PAD = B_PREV_CHUNKS * CHUNK


def _dot(a, b):
    return jnp.dot(a, b, preferred_element_type=F32)


def _dot_nt(a, b):
    return lax.dot_general(a, b, (((1,), (1,)), ((), ())), preferred_element_type=F32)


def _dot_tn(a, b):
    return lax.dot_general(a, b, (((0,), (0,)), ((), ())), preferred_element_type=F32)


def _two_bf16_terms(x):
    hi = x.astype(BF16)
    lo = (x - hi.astype(F32)).astype(BF16)
    return hi, lo


def _log1p_exp_neg_abs(z):
    return jnp.log1p(jnp.exp(-jnp.abs(z)))


def _log_sigmoid(z):
    return jnp.minimum(z, 0.0) - _log1p_exp_neg_abs(z)


def _sigmoid(z):
    return 1.0 / (1.0 + jnp.exp(-z))


def _vmem_limit(pipelined_bytes, resident_bytes):
    want = 2 * pipelined_bytes + resident_bytes
    return int(min(max(want, 16 * 1024 * 1024), VMEM_BYTES_V7X - 8 * 1024 * 1024))


def _params(semantics, pipelined_bytes, resident_bytes):
    return pltpu.CompilerParams(
        dimension_semantics=semantics,
        vmem_limit_bytes=_vmem_limit(pipelined_bytes, resident_bytes))


def _norm_matmul_kernel(x_ref, g_ref, w_ref, o_ref, xn_ref):
    @pl.when(pl.program_id(1) == 0)
    def _():
        x = x_ref[...]
        inv = lax.rsqrt(jnp.mean(x * x, axis=-1, keepdims=True) + EPS)
        xn_ref[...] = (x * inv * g_ref[...]).astype(BF16)

    o_ref[...] = _dot(xn_ref[...], w_ref[...]).astype(o_ref.dtype)


def _norm_matmul(x, g, w, layer, out_dtype, tm, tn):
    t, d = x.shape
    n = w.shape[-1]
    out_bytes = jnp.dtype(out_dtype).itemsize
    return pl.pallas_call(
        _norm_matmul_kernel,
        grid=(t // tm, n // tn),
        in_specs=[
            pl.BlockSpec((tm, d), lambda i, j: (i, 0)),
            pl.BlockSpec((None, 1, d), lambda i, j: (layer, 0, 0)),
            pl.BlockSpec((None, d, tn), lambda i, j: (layer, 0, j)),
        ],
        out_specs=pl.BlockSpec((tm, tn), lambda i, j: (i, j)),
        out_shape=jax.ShapeDtypeStruct((t, n), out_dtype),
        scratch_shapes=[pltpu.VMEM((tm, d), BF16)],
        compiler_params=_params(
            ("parallel", "arbitrary"),
            tm * d * 4 + d * tn * 2 + tm * tn * out_bytes,
            tm * d * 2 + tm * d * 4 + tm * tn * 4),
        name="norm_proj",
    )(x, g, w)


def _gla_kernel(qk_ref, v_ref, g_ref, r_ref, wgk_ref, bgk_ref, ng_ref, o_ref, state_ref, *,
                n_chunks):
    dk_all = A_HEADS * HEAD_DIM
    chunks = range(n_chunks)
    rows = [slice(c * CHUNK, (c + 1) * CHUNK) for c in chunks]

    @pl.when(pl.program_id(1) == 0)
    def _():
        state_ref[...] = jnp.zeros_like(state_ref)

    tri2 = jnp.where(lax.broadcasted_iota(jnp.int32, (CHUNK, 2 * CHUNK), 0)
                     >= lax.broadcasted_iota(jnp.int32, (CHUNK, 2 * CHUNK), 1) % CHUNK,
                     1.0, 0.0).astype(BF16)
    own_head = (lax.broadcasted_iota(jnp.int32, (dk_all, dk_all), 0) // HEAD_DIM
                == lax.broadcasted_iota(jnp.int32, (dk_all, dk_all), 1) // HEAD_DIM)

    def head_stacked(x):
        return jnp.where(own_head, jnp.concatenate([x] * A_HEADS, axis=0), 0.0).astype(BF16)

    r = r_ref[...].astype(BF16)
    gk = _log_sigmoid(_dot(r, wgk_ref[...]) + bgk_ref[...]) * (1.0 / A_GATE_NORM)
    gk_hi, gk_lo = _two_bf16_terms(gk)
    k = qk_ref[:, dk_all:]
    q = qk_ref[:, :dk_all] * (HEAD_DIM ** -0.5)

    k_dec, q_st, v_st, decay = [], [], [], []
    for c in chunks:
        cum = _dot(tri2, jnp.concatenate([gk_hi[rows[c]], gk_lo[rows[c]]], axis=0))
        tot = cum[CHUNK - 1:CHUNK, :]
        k_dec.append(head_stacked(k[rows[c]] * jnp.exp(tot - cum)))
        q_st.append(head_stacked(q[rows[c]]))
        v_c = v_ref[rows[c], :]
        v_st.append(jnp.concatenate([v_c[:, h * A_DV:(h + 1) * A_DV] for h in range(A_HEADS)], axis=0))
        decay.append(jnp.exp(tot))
    kv = [_dot_tn(v_st[c], k_dec[c]) for c in chunks]

    state = state_ref[...]
    outs = []
    for c in chunks:
        state = state * decay[c] + kv[c]
        outs.append(_dot_nt(q_st[c], state.astype(BF16)))
    state_ref[...] = state

    for h in range(A_HEADS):
        cols = slice(h * A_DV, (h + 1) * A_DV)
        o = jnp.concatenate([outs[c][h * CHUNK:(h + 1) * CHUNK] for c in chunks], axis=0)
        o = o * lax.rsqrt(jnp.mean(o * o, axis=-1, keepdims=True) + EPS) * ng_ref[...]
        g = g_ref[:, cols]
        o_ref[:, cols] = (o * (g * _sigmoid(g))).astype(o_ref.dtype)


def _gla(pf, pj, wgk, bgk, ng, layer, tq):
    b, s, _ = pf.shape
    w = BRANCH_WIDTH
    return pl.pallas_call(
        functools.partial(_gla_kernel, n_chunks=tq // CHUNK),
        grid=(b, s // tq),
        in_specs=[
            pl.BlockSpec((None, tq, w), lambda bi, i: (bi, i, F_A_QK)),
            pl.BlockSpec((None, tq, w), lambda bi, i: (bi, i, QKV_A_V)),
            pl.BlockSpec((None, tq, w), lambda bi, i: (bi, i, F_A_G)),
            pl.BlockSpec((None, tq, LANES), lambda bi, i: (bi, i, F_A_R_COL // LANES)),
            pl.BlockSpec((None, LANES, A_HEADS * HEAD_DIM), lambda bi, i: (layer, 0, 0)),
            pl.BlockSpec((None, 1, A_HEADS * HEAD_DIM), lambda bi, i: (layer, 0, 0)),
            pl.BlockSpec((None, 1, A_DV), lambda bi, i: (layer, 0, 0)),
        ],
        out_specs=pl.BlockSpec((None, tq, w), lambda bi, i: (bi, i, 0)),
        out_shape=jax.ShapeDtypeStruct((b, s, w), BF16),
        scratch_shapes=[pltpu.VMEM((A_DV, A_HEADS * HEAD_DIM), F32)],
        compiler_params=_params(("parallel", "arbitrary"), tq * w * 12 + tq * LANES * 4,
                                8 * 1024 * 1024),
        name="mixer_a_gla",
    )(pf, pj, pf, pf, wgk, bgk, ng)


def _chunk_attn_kernel(q_ref, k_ref, v_ref, bias_ref, o_ref, qs_ref, kpad_ref, vfirst_ref,
                       vsecond_ref):
    i = pl.program_id(1)
    s = k_ref.shape[0]
    n_pairs = B_HEADS // 2
    init_rows = 2 * Q_BLOCK

    @pl.when(i == 0)
    def _():
        zeros = jnp.zeros((B_PAD, BRANCH_WIDTH), BF16)
        kpad_ref[0:B_PAD, :] = zeros
        vfirst_ref[0:B_PAD, :] = zeros
        vsecond_ref[0:B_PAD, :] = zeros
        kpad_ref[B_PAD:B_PAD + s, :] = k_ref[...]
        lane = lax.broadcasted_iota(jnp.int32, (init_rows, BRANCH_WIDTH), 1)
        first = (lane % LANES) < HEAD_DIM
        for r0 in range(0, s, init_rows):
            v = v_ref[r0:r0 + init_rows, :].astype(F32)
            vfirst_ref[B_PAD + r0:B_PAD + r0 + init_rows, :] = jnp.where(first, v, 0.0).astype(BF16)
            vsecond_ref[B_PAD + r0:B_PAD + r0 + init_rows, :] = jnp.where(first, 0.0, v).astype(BF16)

    first_head = lax.broadcasted_iota(jnp.int32, (Q_BLOCK, LANES), 1) < HEAD_DIM
    q = q_ref[...].astype(F32) * (HEAD_DIM ** -0.5)
    for p in range(n_pairs):
        q_pair = q[:, p * LANES:(p + 1) * LANES]
        qs_ref[p, 0:Q_BLOCK, :] = jnp.where(first_head, q_pair, 0.0).astype(BF16)
        qs_ref[p, Q_BLOCK:2 * Q_BLOCK, :] = jnp.where(first_head, 0.0, q_pair).astype(BF16)

    start = pl.multiple_of(i * Q_BLOCK, Q_BLOCK)
    window = pl.ds(start, BAND)

    def attend(has_padding):
        pairs = range(n_pairs)
        cols = [slice(p * LANES, (p + 1) * LANES) for p in pairs]
        sc = [_dot_nt(qs_ref[p], kpad_ref[window, cols[p]]) + bias_ref[p] for p in pairs]
        if has_padding:
            in_seq = lax.broadcasted_iota(jnp.int32, (2 * Q_BLOCK, BAND), 1) >= B_PAD - i * Q_BLOCK
            sc = [jnp.where(in_seq, x, MASK_VALUE) for x in sc]
        e = [jnp.exp(x - jnp.max(x, axis=-1, keepdims=True)) for x in sc]
        prob = [(x * (1.0 / jnp.sum(x, axis=-1, keepdims=True))).astype(BF16) for x in e]
        for p in pairs:
            both = jnp.concatenate([prob[p][:Q_BLOCK], prob[p][Q_BLOCK:]], axis=1)
            values = jnp.concatenate([vfirst_ref[window, cols[p]], vsecond_ref[window, cols[p]]], axis=0)
            o_ref[:, cols[p]] = _dot(both, values).astype(o_ref.dtype)

    @pl.when(i < B_PAD // Q_BLOCK)
    def _():
        attend(True)

    @pl.when(i >= B_PAD // Q_BLOCK)
    def _():
        attend(False)


def _chunk_attn(pj, bias, layer):
    b, s, _ = pj.shape
    w = BRANCH_WIDTH
    return pl.pallas_call(
        _chunk_attn_kernel,
        grid=(b, s // Q_BLOCK),
        in_specs=[
            pl.BlockSpec((None, Q_BLOCK, w), lambda bi, i: (bi, i, QKV_B_Q)),
            pl.BlockSpec((None, s, w), lambda bi, i: (bi, 0, QKV_B_K)),
            pl.BlockSpec((None, s, w), lambda bi, i: (bi, 0, QKV_B_V)),
            pl.BlockSpec((None, B_HEADS // 2, 2 * Q_BLOCK, BAND), lambda bi, i: (layer, 0, 0, 0)),
        ],
        out_specs=pl.BlockSpec((None, Q_BLOCK, w), lambda bi, i: (bi, i, 0)),
        out_shape=jax.ShapeDtypeStruct((b, s, w), BF16),
        scratch_shapes=[
            pltpu.VMEM((B_HEADS // 2, 2 * Q_BLOCK, LANES), BF16),
            pltpu.VMEM((s + B_PAD, w), BF16),
            pltpu.VMEM((s + B_PAD, w), BF16),
            pltpu.VMEM((s + B_PAD, w), BF16),
        ],
        compiler_params=_params(
            ("parallel", "arbitrary"),
            2 * Q_BLOCK * w * 2 + 2 * s * w * 2 + B_HEADS * Q_BLOCK * BAND * 4,
            3 * (s + B_PAD) * w * 2 + 16 * 1024 * 1024),
        name="mixer_b_chunk_attn",
    )(pj, pj, pj, bias)


def _band_bias(rel_table):
    depth, heads, _ = rel_table.shape
    width = BAND + Q_BLOCK
    ramp_idx = jnp.clip((BAND - 1) - jnp.arange(width), -B_MAX_REL, B_MAX_REL) + B_MAX_REL
    ramp = rel_table.astype(F32)[:, :, ramp_idx]
    flat = jnp.tile(ramp, (1, 1, Q_BLOCK))[:, :, :Q_BLOCK * (width - 1)]
    skew = flat.reshape(depth, heads, Q_BLOCK, width - 1)
    bias = skew[:, :, :, Q_BLOCK - 1:Q_BLOCK - 1 + BAND]
    qi = jnp.arange(Q_BLOCK)[:, None]
    kj = jnp.arange(BAND)[None, :]
    in_band = (kj // CHUNK >= qi // CHUNK) & (kj // CHUNK <= qi // CHUNK + B_PREV_CHUNKS)
    bias = jnp.where(in_band[None, None], bias, MASK_VALUE)
    return bias.reshape(depth, heads // 2, 2 * Q_BLOCK, BAND)


def _rglru_kernel(g_ref, x_ref, cw_ref, cb_ref, wa_ref, ba_ref, wx_ref, bx_ref, lam_ref, o_ref,
                  xe_ref, a_ref, b_ref, h_ref, carry_ref):
    ts = x_ref.shape[0]
    halo = SUBLANES

    @pl.when(pl.program_id(1) == 0)
    def _():
        xe_ref[0:halo, :] = jnp.zeros((halo, BRANCH_WIDTH), F32)
        carry_ref[...] = jnp.zeros_like(carry_ref)

    x = x_ref[...]
    xe_ref[halo:halo + ts, :] = x
    xc = cb_ref[...]
    for j in range(C_CONV - 1):
        lag = C_CONV - 1 - j
        xc = xc + xe_ref[halo - lag:halo - lag + ts, :] * cw_ref[j:j + 1, :]
    xc = xc + x * cw_ref[C_CONV - 1:C_CONV, :]
    xe_ref[0:halo, :] = xe_ref[ts:ts + halo, :]

    xcb = xc.astype(BF16)
    r = _sigmoid(_dot(xcb, wa_ref[...]) + ba_ref[...])
    gate_i = _sigmoid(_dot(xcb, wx_ref[...]) + bx_ref[...])
    lam = lam_ref[...]
    softplus_neg_lam = jnp.maximum(-lam, 0.0) + _log1p_exp_neg_abs(lam)
    log_a = -C_POW * r * softplus_neg_lam
    a = jnp.exp(log_a)
    a_ref[...] = a
    b_ref[...] = jnp.sqrt(-jnp.tanh(log_a) * (a * a + 1.0)) * (gate_i * xc)

    sub = lax.broadcasted_iota(jnp.int32, (SUBLANES, BRANCH_WIDTH), 0)

    def group(gi, carry):
        rows = pl.ds(pl.multiple_of(gi * SUBLANES, SUBLANES), SUBLANES)
        a = a_ref[rows, :]
        bb = b_ref[rows, :]
        for d in (1, 2, 4):
            a_prev = pltpu.roll(a, d, axis=0)
            b_prev = pltpu.roll(bb, d, axis=0)
            has_prev = sub >= d
            bb = jnp.where(has_prev, a * b_prev + bb, bb)
            a = jnp.where(has_prev, a * a_prev, a)
        h = a * carry + bb
        h_ref[rows, :] = h
        return jnp.broadcast_to(h[SUBLANES - 1:SUBLANES, :], (SUBLANES, BRANCH_WIDTH))

    carry_ref[...] = lax.fori_loop(0, ts // SUBLANES, group, carry_ref[...], unroll=8)
    o_ref[...] = (jax.nn.gelu(g_ref[...]) * h_ref[...]).astype(o_ref.dtype)


def _rglru(pf, cw, cb, wa, ba, wx, bx, lam, layer, ts):
    b, s, _ = pf.shape
    w = BRANCH_WIDTH
    vec = pl.BlockSpec((None, 1, w), lambda bi, i: (layer, 0, 0))
    mat = pl.BlockSpec((None, w, w), lambda bi, i: (layer, 0, 0))
    return pl.pallas_call(
        _rglru_kernel,
        grid=(b, s // ts),
        in_specs=[
            pl.BlockSpec((None, ts, w), lambda bi, i: (bi, i, F_C_G)),
            pl.BlockSpec((None, ts, w), lambda bi, i: (bi, i, F_C_X)),
            pl.BlockSpec((None, C_CONV, w), lambda bi, i: (layer, 0, 0)),
            vec, mat, vec, mat, vec, vec,
        ],
        out_specs=pl.BlockSpec((None, ts, w), lambda bi, i: (bi, i, 0)),
        out_shape=jax.ShapeDtypeStruct((b, s, w), BF16),
        scratch_shapes=[
            pltpu.VMEM((ts + 2 * SUBLANES, w), F32),
            pltpu.VMEM((ts, w), F32),
            pltpu.VMEM((ts, w), F32),
            pltpu.VMEM((ts, w), F32),
            pltpu.VMEM((SUBLANES, w), F32),
        ],
        compiler_params=_params(("parallel", "arbitrary"), ts * w * 10 + 2 * w * w * 2,
                                4 * ts * w * 4 + 8 * 1024 * 1024),
        name="mixer_c_rglru",
    )(pf, pf, cw, cb, wa, ba, wx, bx, lam)


def _block_diag(wb):
    l, g, n, _ = wb.shape
    eye = jnp.eye(g, dtype=wb.dtype)
    return (wb[:, :, :, None, :] * eye[None, :, None, :, None]).reshape(l, g * n, g * n)


def _stick_kernel(q_ref, k_ref, v_ref, o_ref, qs_ref, vs_ref, later_ref, acc_ref):
    i = pl.program_id(1)
    n_pairs = D_HEADS // 2
    n_key_blocks = k_ref.shape[0] // Q_BLOCK
    first_head = lax.broadcasted_iota(jnp.int32, (Q_BLOCK, LANES), 1) < HEAD_DIM

    @pl.when(i == 0)
    def _():
        for jb in range(n_key_blocks):
            for p in range(n_pairs):
                v = v_ref[jb * Q_BLOCK:(jb + 1) * Q_BLOCK, p * LANES:(p + 1) * LANES].astype(F32)
                vs_ref[jb, p, 0:Q_BLOCK, :] = jnp.where(first_head, v, 0.0).astype(BF16)
                vs_ref[jb, p, Q_BLOCK:2 * Q_BLOCK, :] = jnp.where(first_head, 0.0, v).astype(BF16)

    q = q_ref[...].astype(F32) * (HEAD_DIM ** -0.5)
    for p in range(n_pairs):
        q_pair = q[:, p * LANES:(p + 1) * LANES]
        qs_ref[p, 0:Q_BLOCK, :] = jnp.where(first_head, q_pair, 0.0).astype(BF16)
        qs_ref[p, Q_BLOCK:2 * Q_BLOCK, :] = jnp.where(first_head, 0.0, q_pair).astype(BF16)

    row = lax.broadcasted_iota(jnp.int32, (2 * Q_BLOCK, Q_BLOCK), 0) % Q_BLOCK
    col = lax.broadcasted_iota(jnp.int32, (2 * Q_BLOCK, Q_BLOCK), 1)
    strictly_before = col < row
    rr = lax.broadcasted_iota(jnp.int32, (2 * Q_BLOCK, 2 * Q_BLOCK), 0) % Q_BLOCK
    cc = lax.broadcasted_iota(jnp.int32, (2 * Q_BLOCK, 2 * Q_BLOCK), 1)
    suffix_and_total = jnp.where((rr > cc) | (cc >= Q_BLOCK), 1.0, 0.0).astype(BF16)

    def key_blocks(js, from_diagonal):
        pairs = range(n_pairs)
        blocks = range(len(js))
        ks = [pl.multiple_of(j * Q_BLOCK, Q_BLOCK) for j in js]
        both = [(b, p) for b in blocks for p in pairs]
        z, neg_log1m, log_sig, sums, att = {}, {}, {}, {}, {}
        for b, p in both:
            z[b, p] = _dot_nt(qs_ref[p], k_ref[pl.ds(ks[b], Q_BLOCK), p * LANES:(p + 1) * LANES])
        for b, p in both:
            t = jnp.maximum(z[b, p], 0.0) + jnp.log(1.0 + jnp.exp(-jnp.abs(z[b, p])))
            log_sig[b, p] = z[b, p] - t
            masked = from_diagonal and b == 0
            neg_log1m[b, p] = jnp.where(strictly_before, t, 0.0) if masked else t
        for b, p in both:
            hi, lo = _two_bf16_terms(neg_log1m[b, p])
            sums[b, p] = _dot(jnp.concatenate([hi, lo], axis=1), suffix_and_total)
        tails, totals = {}, []
        for p in pairs:
            later = None if from_diagonal else later_ref[p]
            for b in blocks:
                tail = sums[b, p][:, :Q_BLOCK]
                total = sums[b, p][:, Q_BLOCK:]
                if later is not None:
                    tail = tail + later
                    total = total + later
                tails[b, p] = tail
                later = total
            later_ref[p] = later
            totals.append(later)
        least = jnp.min(jnp.minimum(jnp.minimum(totals[0], totals[1]), jnp.minimum(totals[2], totals[3])))
        for b, p in both:
            a = jnp.exp(log_sig[b, p] - tails[b, p])
            if from_diagonal and b == 0:
                a = jnp.where(strictly_before, a, 0.0)
            a = a.astype(BF16)
            att[b, p] = jnp.concatenate([a[:Q_BLOCK], a[Q_BLOCK:]], axis=1)
        for p in pairs:
            pv = None if from_diagonal else acc_ref[p]
            for b in blocks:
                term = _dot(att[b, p], vs_ref[js[b], p])
                pv = term if pv is None else pv + term
            acc_ref[p] = pv
        return least

    least = lax.cond(i > 0, lambda: key_blocks([i, i - 1], True), lambda: key_blocks([i], True))

    def more(carry):
        jj, least = carry
        return jnp.logical_and(jj < i, least < EXP_UNDERFLOW)

    def earlier(carry):
        jj, _ = carry
        return jj + 1, key_blocks([i - 1 - jj], False)

    lax.while_loop(more, earlier, (jnp.int32(1), least))

    for p in range(n_pairs):
        o_ref[:, p * LANES:(p + 1) * LANES] = acc_ref[p].astype(o_ref.dtype)


def _stick(pj):
    b, s, _ = pj.shape
    w = BRANCH_WIDTH
    return pl.pallas_call(
        _stick_kernel,
        grid=(b, s // Q_BLOCK),
        in_specs=[
            pl.BlockSpec((None, Q_BLOCK, w), lambda bi, i: (bi, i, QKV_D_Q)),
            pl.BlockSpec((None, s, w), lambda bi, i: (bi, 0, QKV_D_K)),
            pl.BlockSpec((None, s, w), lambda bi, i: (bi, 0, QKV_D_V)),
        ],
        out_specs=pl.BlockSpec((None, Q_BLOCK, w), lambda bi, i: (bi, i, 0)),
        out_shape=jax.ShapeDtypeStruct((b, s, w), BF16),
        scratch_shapes=[
            pltpu.VMEM((D_HEADS // 2, 2 * Q_BLOCK, LANES), BF16),
            pltpu.VMEM((s // Q_BLOCK, D_HEADS // 2, 2 * Q_BLOCK, LANES), BF16),
            pltpu.VMEM((D_HEADS // 2, 2 * Q_BLOCK, LANES), F32),
            pltpu.VMEM((D_HEADS // 2, Q_BLOCK, LANES), F32),
        ],
        compiler_params=_params(("parallel", "arbitrary"), 2 * Q_BLOCK * w * 2 + 2 * s * w * 2,
                                2 * s * w * 2 + 8 * 1024 * 1024),
        name="mixer_d_stick",
    )(pj, pj, pj)


def _merge_kernel(ya_ref, yb_ref, yc_ref, yd_ref, gl_ref, h_ref, wb_ref, wo_ref, o_ref):
    d = h_ref.shape[1]
    mixed = None
    for n, y_ref in enumerate((ya_ref, yb_ref, yc_ref, yd_ref)):
        term = _sigmoid(gl_ref[:, n * d:(n + 1) * d]) * _dot(y_ref[...], wb_ref[n])
        mixed = term if mixed is None else mixed + term
    o_ref[...] = h_ref[...] + _dot(mixed.astype(BF16), wo_ref[...])


def _merge(ya, yb, yc, yd, pf, h, wb, wo, layer, tm):
    t, d = h.shape
    w = BRANCH_WIDTH
    y_spec = pl.BlockSpec((tm, w), lambda i: (i, 0))
    return pl.pallas_call(
        _merge_kernel,
        grid=(t // tm,),
        in_specs=[
            y_spec, y_spec, y_spec, y_spec,
            pl.BlockSpec((tm, N_BRANCH * d), lambda i: (i, 0)),
            pl.BlockSpec((tm, d), lambda i: (i, 0)),
            pl.BlockSpec((None, N_BRANCH, w, d), lambda i: (layer, 0, 0, 0)),
            pl.BlockSpec((None, d, d), lambda i: (layer, 0, 0)),
        ],
        out_specs=pl.BlockSpec((tm, d), lambda i: (i, 0)),
        out_shape=jax.ShapeDtypeStruct((t, d), F32),
        compiler_params=_params(
            ("parallel",),
            4 * tm * w * 2 + tm * N_BRANCH * d * 4 + 2 * tm * d * 4 + (N_BRANCH * w * d + d * d) * 2,
            4 * tm * d * 4),
        name="merge",
    )(ya, yb, yc, yd, pf, h, wb, wo)


def _ffn_kernel(h_ref, g_ref, wg_ref, wu_ref, wd_ref, gf_ref, o_ref, *, hidden_chunk, final_norm):
    h = h_ref[...]
    hn = (h * lax.rsqrt(jnp.mean(h * h, axis=-1, keepdims=True) + EPS) * g_ref[...]).astype(BF16)
    hidden = wg_ref.shape[1]
    acc = h
    for c0 in range(0, hidden, hidden_chunk):
        cs = slice(c0, min(c0 + hidden_chunk, hidden))
        gate = _dot(hn, wg_ref[:, cs])
        up = _dot(hn, wu_ref[:, cs])
        act = (gate * _sigmoid(gate) * up).astype(BF16)
        acc = acc + _dot(act, wd_ref[cs, :])
    if final_norm:
        acc = acc * lax.rsqrt(jnp.mean(acc * acc, axis=-1, keepdims=True) + EPS) * gf_ref[...]
    o_ref[...] = acc


def _ffn(h, g, wg, wu, wd, gf, layer, tm, final_norm):
    t, d = h.shape
    hidden = wg.shape[-1]
    return pl.pallas_call(
        functools.partial(_ffn_kernel, hidden_chunk=512, final_norm=final_norm),
        grid=(t // tm,),
        in_specs=[
            pl.BlockSpec((tm, d), lambda i: (i, 0)),
            pl.BlockSpec((None, 1, d), lambda i: (layer, 0, 0)),
            pl.BlockSpec((None, d, hidden), lambda i: (layer, 0, 0)),
            pl.BlockSpec((None, d, hidden), lambda i: (layer, 0, 0)),
            pl.BlockSpec((None, hidden, d), lambda i: (layer, 0, 0)),
            pl.BlockSpec((1, d), lambda i: (0, 0)),
        ],
        out_specs=pl.BlockSpec((tm, d), lambda i: (i, 0)),
        out_shape=jax.ShapeDtypeStruct((t, d), F32),
        compiler_params=_params(("parallel",), 2 * tm * d * 4 + 3 * d * hidden * 2,
                                3 * tm * d * 4 + 3 * tm * 512 * 4),
        name="ffn",
    )(h, g, wg, wu, wd, gf)


def kernel(x, norm_mix, w_in, a_w_gk, a_b_gk, a_norm, b_rel_bias, c_conv_w, c_conv_b, c_w_a, c_b_a,
           c_w_x, c_b_x, c_lambda, w_branch, w_out, norm_ffn, w_ffn_gate, w_ffn_up, w_ffn_down,
           norm_final):
    bsz, s, d = x.shape
    depth = w_in.shape[0]
    t = bsz * s
    w = BRANCH_WIDTH

    sizes = (A_HEADS * HEAD_DIM, A_HEADS * HEAD_DIM, A_HEADS * A_DV, A_RANK, A_HEADS * A_DV,
             w, w, w, w, w, w, w, w, N_BRANCH * d)
    offs = [0]
    for sz in sizes:
        offs.append(offs[-1] + sz)
    (o_aq, o_ak, o_av, o_ar, o_ag, o_bq, _, _, o_cg, _, o_dq, _, _, o_gate, o_end) = offs

    w_in_b = w_in.astype(BF16)
    w_qkv = jnp.concatenate(
        [w_in_b[:, :, o_av:o_ar], w_in_b[:, :, o_bq:o_cg], w_in_b[:, :, o_dq:o_gate]], axis=-1)
    w_f = jnp.concatenate(
        [w_in_b[:, :, o_gate:o_end], w_in_b[:, :, o_ag:o_bq], w_in_b[:, :, o_cg:o_dq],
         w_in_b[:, :, o_aq:o_av], w_in_b[:, :, o_ar:o_ag],
         jnp.zeros((depth, d, 2 * LANES - A_RANK), BF16)], axis=-1)
    wgk = jnp.pad(a_w_gk, ((0, 0), (0, LANES - A_RANK), (0, 0))).astype(BF16)
    vec = lambda p: p.reshape(depth, 1, -1)
    bias = _band_bias(b_rel_bias)
    wa = _block_diag(c_w_a).astype(BF16)
    wx = _block_diag(c_w_x).astype(BF16)
    wb = w_branch.astype(BF16)
    wo = w_out.astype(BF16)
    wg = w_ffn_gate.astype(BF16)
    wu = w_ffn_up.astype(BF16)
    wd = w_ffn_down.astype(BF16)
    gf = norm_final.reshape(1, d)

    h = x.reshape(t, d)
    tm_proj = min(2048, t)
    for layer in range(depth):
        pj = _norm_matmul(h, vec(norm_mix), w_qkv, layer, BF16, tm_proj, 896).reshape(bsz, s, QKV_COLS)
        pf = _norm_matmul(h, vec(norm_mix), w_f, layer, F32, tm_proj, 640).reshape(bsz, s, F_COLS)
        ya = _gla(pf, pj, wgk, vec(a_b_gk), vec(a_norm), layer, 512)
        yb = _chunk_attn(pj, bias, layer)
        yc = _rglru(pf, c_conv_w, vec(c_conv_b), wa, vec(c_b_a), wx, vec(c_b_x), vec(c_lambda),
                    layer, 512)
        yd = _stick(pj)
        h = _merge(ya.reshape(t, w), yb.reshape(t, w), yc.reshape(t, w), yd.reshape(t, w),
                   pf.reshape(t, F_COLS), h, wb, wo, layer, 512)
        h = _ffn(h, vec(norm_ffn), wg, wu, wd, gf, layer, 512, layer == depth - 1)
    return h.reshape(bsz, s, d)
```

```python
import functools

import jax
import jax.numpy as jnp
from jax import lax
from jax.experimental import pallas as pl
from jax.experimental.pallas import tpu as pltpu

F32 = jnp.float32
BF16 = jnp.bfloat16

EPS = 1e-6
CHUNK = 64
BRANCH_WIDTH = 512
N_BRANCH = 4
HEAD_DIM = 64
A_HEADS = 4
A_DV = 128
A_RANK = 16
A_GATE_NORM = 16.0
B_HEADS = 8
B_PREV_CHUNKS = 8
B_MAX_REL = 128
C_BLOCKS = 8
C_CONV = 4
C_POW = 8.0
D_HEADS = 8
MASK_VALUE = -1e30
EXP_UNDERFLOW = 104.0

LANES = 128
SUBLANES = 8
VMEM_BYTES_V7X = 64 * 1024 * 1024

QKV_COLS = 7 * BRANCH_WIDTH
QKV_A_V, QKV_B_Q, QKV_B_K, QKV_B_V, QKV_D_Q, QKV_D_K, QKV_D_V = range(7)
F_A_G, F_C_G, F_C_X, F_A_QK = range(4)
F_A_R_COL = 4 * BRANCH_WIDTH
F_COLS = F_A_R_COL + 2 * LANES

Q_BLOCK = 128
BAND = (B_PREV_CHUNKS + 2) * CHUNK
B_PAD = B_PREV_CHUNKS * CHUNK


def _dot(a, b):
    return jnp.dot(a, b, preferred_element_type=F32)


def _dot_nt(a, b):
    return lax.dot_general(a, b, (((1,), (1,)), ((), ())), preferred_element_type=F32)


def _dot_tn(a, b):
    return lax.dot_general(a, b, (((0,), (0,)), ((), ())), preferred_element_type=F32)


def _two_bf16_terms(x):
    hi = x.astype(BF16)
    lo = (x - hi.astype(F32)).astype(BF16)
    return hi, lo


def _log1p_exp_neg_abs(z):
    return jnp.log1p(jnp.exp(-jnp.abs(z)))


def _log_sigmoid(z):
    return jnp.minimum(z, 0.0) - _log1p_exp_neg_abs(z)


def _sigmoid(z):
    return 1.0 / (1.0 + jnp.exp(-z))


def _vmem_limit(pipelined_bytes, resident_bytes):
    want = 2 * pipelined_bytes + resident_bytes
    return int(min(max(want, 16 * 1024 * 1024), VMEM_BYTES_V7X - 8 * 1024 * 1024))


def _params(semantics, pipelined_bytes, resident_bytes):
    return pltpu.CompilerParams(
        dimension_semantics=semantics,
        vmem_limit_bytes=_vmem_limit(pipelined_bytes, resident_bytes))


def _rmsnorm(x, g):
    return x * lax.rsqrt(jnp.mean(x * x, axis=-1, keepdims=True) + EPS) * g


def _norm_kernel(x_ref, g_ref, o_ref):
    o_ref[...] = _rmsnorm(x_ref[...], g_ref[...]).astype(o_ref.dtype)


def _norm(x, g, layer, tm):
    t, d = x.shape
    return pl.pallas_call(
        _norm_kernel,
        grid=(t // tm,),
        in_specs=[
            pl.BlockSpec((tm, d), lambda i: (i, 0)),
            pl.BlockSpec((None, 1, d), lambda i: (layer, 0, 0)),
        ],
        out_specs=pl.BlockSpec((tm, d), lambda i: (i, 0)),
        out_shape=jax.ShapeDtypeStruct((t, d), BF16),
        compiler_params=_params(("parallel",), tm * d * 6, 2 * tm * d * 4),
        name="norm",
    )(x, g)


def _proj_kernel(x_ref, w_ref, o_ref):
    o_ref[...] = _dot(x_ref[...], w_ref[...]).astype(o_ref.dtype)


def _proj(xn, w, layer, out_dtype, tm, tn):
    t, d = xn.shape
    n = w.shape[-1]
    out_bytes = jnp.dtype(out_dtype).itemsize
    return pl.pallas_call(
        _proj_kernel,
        grid=(t // tm, n // tn),
        in_specs=[
            pl.BlockSpec((tm, d), lambda i, j: (i, 0)),
            pl.BlockSpec((None, d, tn), lambda i, j: (layer, 0, j)),
        ],
        out_specs=pl.BlockSpec((tm, tn), lambda i, j: (i, j)),
        out_shape=jax.ShapeDtypeStruct((t, n), out_dtype),
        compiler_params=_params(
            ("parallel", "arbitrary"),
            tm * d * 2 + d * tn * 2 + tm * tn * out_bytes,
            tm * tn * 4),
        name="proj",
    )(xn, w)


def _gla_kernel(qk_ref, v_ref, g_ref, r_ref, wgk_ref, bgk_ref, ng_ref, o_ref, state_ref, *,
                n_chunks):
    dk_all = A_HEADS * HEAD_DIM
    chunks = range(n_chunks)
    rows = [slice(c * CHUNK, (c + 1) * CHUNK) for c in chunks]

    @pl.when(pl.program_id(1) == 0)
    def _():
        state_ref[...] = jnp.zeros_like(state_ref)

    tri2 = jnp.where(lax.broadcasted_iota(jnp.int32, (CHUNK, 2 * CHUNK), 0)
                     >= lax.broadcasted_iota(jnp.int32, (CHUNK, 2 * CHUNK), 1) % CHUNK,
                     1.0, 0.0).astype(BF16)
    own_head = (lax.broadcasted_iota(jnp.int32, (dk_all, dk_all), 0) // HEAD_DIM
                == lax.broadcasted_iota(jnp.int32, (dk_all, dk_all), 1) // HEAD_DIM)

    def head_stacked(x):
        return jnp.where(own_head, jnp.concatenate([x] * A_HEADS, axis=0), 0.0).astype(BF16)

    r = r_ref[...].astype(BF16)
    gk = _log_sigmoid(_dot(r, wgk_ref[...]) + bgk_ref[...]) * (1.0 / A_GATE_NORM)
    gk_hi, gk_lo = _two_bf16_terms(gk)
    k = qk_ref[:, dk_all:]
    q = qk_ref[:, :dk_all] * (HEAD_DIM ** -0.5)

    k_dec, q_st, v_st, decay = [], [], [], []
    for c in chunks:
        cum = _dot(tri2, jnp.concatenate([gk_hi[rows[c]], gk_lo[rows[c]]], axis=0))
        tot = cum[CHUNK - 1:CHUNK, :]
        k_dec.append(head_stacked(k[rows[c]] * jnp.exp(tot - cum)))
        q_st.append(head_stacked(q[rows[c]]))
        v_c = v_ref[rows[c], :]
        v_st.append(jnp.concatenate([v_c[:, h * A_DV:(h + 1) * A_DV] for h in range(A_HEADS)], axis=0))
        decay.append(jnp.exp(tot))
    kv = [_dot_tn(v_st[c], k_dec[c]) for c in chunks]

    state = state_ref[...]
    outs = []
    for c in chunks:
        state = state * decay[c] + kv[c]
        outs.append(_dot_nt(q_st[c], state.astype(BF16)))
    state_ref[...] = state

    for h in range(A_HEADS):
        cols = slice(h * A_DV, (h + 1) * A_DV)
        o = jnp.concatenate([outs[c][h * CHUNK:(h + 1) * CHUNK] for c in chunks], axis=0)
        o = o * lax.rsqrt(jnp.mean(o * o, axis=-1, keepdims=True) + EPS) * ng_ref[...]
        g = g_ref[:, cols]
        o_ref[:, cols] = (o * (g * _sigmoid(g))).astype(o_ref.dtype)


def _gla(pf, pj, wgk, bgk, ng, layer, tq):
    b, s, _ = pf.shape
    w = BRANCH_WIDTH
    return pl.pallas_call(
        functools.partial(_gla_kernel, n_chunks=tq // CHUNK),
        grid=(b, s // tq),
        in_specs=[
            pl.BlockSpec((None, tq, w), lambda bi, i: (bi, i, F_A_QK)),
            pl.BlockSpec((None, tq, w), lambda bi, i: (bi, i, QKV_A_V)),
            pl.BlockSpec((None, tq, w), lambda bi, i: (bi, i, F_A_G)),
            pl.BlockSpec((None, tq, LANES), lambda bi, i: (bi, i, F_A_R_COL // LANES)),
            pl.BlockSpec((None, LANES, A_HEADS * HEAD_DIM), lambda bi, i: (layer, 0, 0)),
            pl.BlockSpec((None, 1, A_HEADS * HEAD_DIM), lambda bi, i: (layer, 0, 0)),
            pl.BlockSpec((None, 1, A_DV), lambda bi, i: (layer, 0, 0)),
        ],
        out_specs=pl.BlockSpec((None, tq, w), lambda bi, i: (bi, i, 0)),
        out_shape=jax.ShapeDtypeStruct((b, s, w), BF16),
        scratch_shapes=[pltpu.VMEM((A_DV, A_HEADS * HEAD_DIM), F32)],
        compiler_params=_params(("parallel", "arbitrary"), tq * w * 12 + tq * LANES * 4,
                                8 * 1024 * 1024),
        name="mixer_a_gla",
    )(pf, pj, pf, pf, wgk, bgk, ng)


def _chunk_attn_kernel(q_ref, k_ref, v_ref, bias_ref, o_ref, qs_ref, kpad_ref, vfirst_ref,
                       vsecond_ref):
    i = pl.program_id(1)
    s = k_ref.shape[0]
    n_pairs = B_HEADS // 2
    init_rows = 2 * Q_BLOCK

    @pl.when(i == 0)
    def _():
        zeros = jnp.zeros((B_PAD, BRANCH_WIDTH), BF16)
        kpad_ref[0:B_PAD, :] = zeros
        vfirst_ref[0:B_PAD, :] = zeros
        vsecond_ref[0:B_PAD, :] = zeros
        kpad_ref[B_PAD:B_PAD + s, :] = k_ref[...]
        lane = lax.broadcasted_iota(jnp.int32, (init_rows, BRANCH_WIDTH), 1)
        first = (lane % LANES) < HEAD_DIM
        for r0 in range(0, s, init_rows):
            v = v_ref[r0:r0 + init_rows, :].astype(F32)
            vfirst_ref[B_PAD + r0:B_PAD + r0 + init_rows, :] = jnp.where(first, v, 0.0).astype(BF16)
            vsecond_ref[B_PAD + r0:B_PAD + r0 + init_rows, :] = jnp.where(first, 0.0, v).astype(BF16)

    first_head = lax.broadcasted_iota(jnp.int32, (Q_BLOCK, LANES), 1) < HEAD_DIM
    q = q_ref[...].astype(F32) * (HEAD_DIM ** -0.5)
    for p in range(n_pairs):
        q_pair = q[:, p * LANES:(p + 1) * LANES]
        qs_ref[p, 0:Q_BLOCK, :] = jnp.where(first_head, q_pair, 0.0).astype(BF16)
        qs_ref[p, Q_BLOCK:2 * Q_BLOCK, :] = jnp.where(first_head, 0.0, q_pair).astype(BF16)

    start = pl.multiple_of(i * Q_BLOCK, Q_BLOCK)
    window = pl.ds(start, BAND)

    def attend(has_padding):
        pairs = range(n_pairs)
        cols = [slice(p * LANES, (p + 1) * LANES) for p in pairs]
        sc = [_dot_nt(qs_ref[p], kpad_ref[window, cols[p]]) + bias_ref[p] for p in pairs]
        if has_padding:
            in_seq = lax.broadcasted_iota(jnp.int32, (2 * Q_BLOCK, BAND), 1) >= B_PAD - i * Q_BLOCK
            sc = [jnp.where(in_seq, x, MASK_VALUE) for x in sc]
        e = [jnp.exp(x - jnp.max(x, axis=-1, keepdims=True)) for x in sc]
        prob = [(x * (1.0 / jnp.sum(x, axis=-1, keepdims=True))).astype(BF16) for x in e]
        for p in pairs:
            both = jnp.concatenate([prob[p][:Q_BLOCK], prob[p][Q_BLOCK:]], axis=1)
            values = jnp.concatenate([vfirst_ref[window, cols[p]], vsecond_ref[window, cols[p]]], axis=0)
            o_ref[:, cols[p]] = _dot(both, values).astype(o_ref.dtype)

    @pl.when(i < B_PAD // Q_BLOCK)
    def _():
        attend(True)

    @pl.when(i >= B_PAD // Q_BLOCK)
    def _():
        attend(False)


def _chunk_attn(pj, bias, layer):
    b, s, _ = pj.shape
    w = BRANCH_WIDTH
    return pl.pallas_call(
        _chunk_attn_kernel,
        grid=(b, s // Q_BLOCK),
        in_specs=[
            pl.BlockSpec((None, Q_BLOCK, w), lambda bi, i: (bi, i, QKV_B_Q)),
            pl.BlockSpec((None, s, w), lambda bi, i: (bi, 0, QKV_B_K)),
            pl.BlockSpec((None, s, w), lambda bi, i: (bi, 0, QKV_B_V)),
            pl.BlockSpec((None, B_HEADS // 2, 2 * Q_BLOCK, BAND), lambda bi, i: (layer, 0, 0, 0)),
        ],
        out_specs=pl.BlockSpec((None, Q_BLOCK, w), lambda bi, i: (bi, i, 0)),
        out_shape=jax.ShapeDtypeStruct((b, s, w), BF16),
        scratch_shapes=[
            pltpu.VMEM((B_HEADS // 2, 2 * Q_BLOCK, LANES), BF16),
            pltpu.VMEM((s + B_PAD, w), BF16),
            pltpu.VMEM((s + B_PAD, w), BF16),
            pltpu.VMEM((s + B_PAD, w), BF16),
        ],
        compiler_params=_params(
            ("parallel", "arbitrary"),
            2 * Q_BLOCK * w * 2 + 2 * s * w * 2 + B_HEADS * Q_BLOCK * BAND * 4,
            3 * (s + B_PAD) * w * 2 + 16 * 1024 * 1024),
        name="mixer_b_chunk_attn",
    )(pj, pj, pj, bias)


def _band_bias(rel_table):
    depth, heads, _ = rel_table.shape
    width = BAND + Q_BLOCK
    ramp_idx = jnp.clip((BAND - 1) - jnp.arange(width), -B_MAX_REL, B_MAX_REL) + B_MAX_REL
    ramp = rel_table.astype(F32)[:, :, ramp_idx]
    flat = jnp.tile(ramp, (1, 1, Q_BLOCK))[:, :, :Q_BLOCK * (width - 1)]
    skew = flat.reshape(depth, heads, Q_BLOCK, width - 1)
    bias = skew[:, :, :, Q_BLOCK - 1:Q_BLOCK - 1 + BAND]
    qi = jnp.arange(Q_BLOCK)[:, None]
    kj = jnp.arange(BAND)[None, :]
    in_band = (kj // CHUNK >= qi // CHUNK) & (kj // CHUNK <= qi // CHUNK + B_PREV_CHUNKS)
    bias = jnp.where(in_band[None, None], bias, MASK_VALUE)
    return bias.reshape(depth, heads // 2, 2 * Q_BLOCK, BAND)


def _rglru_kernel(g_ref, x_ref, cw_ref, cb_ref, wa_ref, ba_ref, wx_ref, bx_ref, lam_ref, o_ref,
                  xe_ref, a_ref, b_ref, h_ref, carry_ref):
    ts = x_ref.shape[0]
    halo = SUBLANES

    @pl.when(pl.program_id(1) == 0)
    def _():
        xe_ref[0:halo, :] = jnp.zeros((halo, BRANCH_WIDTH), F32)
        carry_ref[...] = jnp.zeros_like(carry_ref)

    x = x_ref[...]
    xe_ref[halo:halo + ts, :] = x
    xc = cb_ref[...]
    for j in range(C_CONV - 1):
        lag = C_CONV - 1 - j
        xc = xc + xe_ref[halo - lag:halo - lag + ts, :] * cw_ref[j:j + 1, :]
    xc = xc + x * cw_ref[C_CONV - 1:C_CONV, :]
    xe_ref[0:halo, :] = xe_ref[ts:ts + halo, :]

    xcb = xc.astype(BF16)
    r = _sigmoid(_dot(xcb, wa_ref[...]) + ba_ref[...])
    gate_i = _sigmoid(_dot(xcb, wx_ref[...]) + bx_ref[...])
    lam = lam_ref[...]
    softplus_neg_lam = jnp.maximum(-lam, 0.0) + _log1p_exp_neg_abs(lam)
    log_a = -C_POW * r * softplus_neg_lam
    a = jnp.exp(log_a)
    a_ref[...] = a
    b_ref[...] = jnp.sqrt(-jnp.tanh(log_a) * (a * a + 1.0)) * (gate_i * xc)

    sub = lax.broadcasted_iota(jnp.int32, (SUBLANES, BRANCH_WIDTH), 0)

    def group(gi, carry):
        rows = pl.ds(pl.multiple_of(gi * SUBLANES, SUBLANES), SUBLANES)
        a = a_ref[rows, :]
        bb = b_ref[rows, :]
        for d in (1, 2, 4):
            a_prev = pltpu.roll(a, d, axis=0)
            b_prev = pltpu.roll(bb, d, axis=0)
            has_prev = sub >= d
            bb = jnp.where(has_prev, a * b_prev + bb, bb)
            a = jnp.where(has_prev, a * a_prev, a)
        h = a * carry + bb
        h_ref[rows, :] = h
        return jnp.broadcast_to(h[SUBLANES - 1:SUBLANES, :], (SUBLANES, BRANCH_WIDTH))

    carry_ref[...] = lax.fori_loop(0, ts // SUBLANES, group, carry_ref[...], unroll=8)
    o_ref[...] = (jax.nn.gelu(g_ref[...]) * h_ref[...]).astype(o_ref.dtype)


def _rglru(pf, cw, cb, wa, ba, wx, bx, lam, layer, ts):
    b, s, _ = pf.shape
    w = BRANCH_WIDTH
    vec = pl.BlockSpec((None, 1, w), lambda bi, i: (layer, 0, 0))
    mat = pl.BlockSpec((None, w, w), lambda bi, i: (layer, 0, 0))
    return pl.pallas_call(
        _rglru_kernel,
        grid=(b, s // ts),
        in_specs=[
            pl.BlockSpec((None, ts, w), lambda bi, i: (bi, i, F_C_G)),
            pl.BlockSpec((None, ts, w), lambda bi, i: (bi, i, F_C_X)),
            pl.BlockSpec((None, C_CONV, w), lambda bi, i: (layer, 0, 0)),
            vec, mat, vec, mat, vec, vec,
        ],
        out_specs=pl.BlockSpec((None, ts, w), lambda bi, i: (bi, i, 0)),
        out_shape=jax.ShapeDtypeStruct((b, s, w), BF16),
        scratch_shapes=[
            pltpu.VMEM((ts + 2 * SUBLANES, w), F32),
            pltpu.VMEM((ts, w), F32),
            pltpu.VMEM((ts, w), F32),
            pltpu.VMEM((ts, w), F32),
            pltpu.VMEM((SUBLANES, w), F32),
        ],
        compiler_params=_params(("parallel", "arbitrary"), ts * w * 10 + 2 * w * w * 2,
                                4 * ts * w * 4 + 8 * 1024 * 1024),
        name="mixer_c_rglru",
    )(pf, pf, cw, cb, wa, ba, wx, bx, lam)


def _block_diag(wb):
    l, g, n, _ = wb.shape
    eye = jnp.eye(g, dtype=wb.dtype)
    return (wb[:, :, :, None, :] * eye[None, :, None, :, None]).reshape(l, g * n, g * n)


def _stick_kernel(q_ref, k_ref, v_ref, o_ref, qs_ref, vs_ref, later_ref, acc_ref):
    i = pl.program_id(1)
    n_pairs = D_HEADS // 2
    n_key_blocks = k_ref.shape[0] // Q_BLOCK
    first_head = lax.broadcasted_iota(jnp.int32, (Q_BLOCK, LANES), 1) < HEAD_DIM

    @pl.when(i == 0)
    def _():
        for jb in range(n_key_blocks):
            for p in range(n_pairs):
                v = v_ref[jb * Q_BLOCK:(jb + 1) * Q_BLOCK, p * LANES:(p + 1) * LANES].astype(F32)
                vs_ref[jb, p, 0:Q_BLOCK, :] = jnp.where(first_head, v, 0.0).astype(BF16)
                vs_ref[jb, p, Q_BLOCK:2 * Q_BLOCK, :] = jnp.where(first_head, 0.0, v).astype(BF16)

    q = q_ref[...].astype(F32) * (HEAD_DIM ** -0.5)
    for p in range(n_pairs):
        q_pair = q[:, p * LANES:(p + 1) * LANES]
        qs_ref[p, 0:Q_BLOCK, :] = jnp.where(first_head, q_pair, 0.0).astype(BF16)
        qs_ref[p, Q_BLOCK:2 * Q_BLOCK, :] = jnp.where(first_head, 0.0, q_pair).astype(BF16)

    row = lax.broadcasted_iota(jnp.int32, (2 * Q_BLOCK, Q_BLOCK), 0) % Q_BLOCK
    col = lax.broadcasted_iota(jnp.int32, (2 * Q_BLOCK, Q_BLOCK), 1)
    strictly_before = col < row
    rr = lax.broadcasted_iota(jnp.int32, (2 * Q_BLOCK, 2 * Q_BLOCK), 0) % Q_BLOCK
    cc = lax.broadcasted_iota(jnp.int32, (2 * Q_BLOCK, 2 * Q_BLOCK), 1)
    suffix_and_total = jnp.where((rr > cc) | (cc >= Q_BLOCK), 1.0, 0.0).astype(BF16)

    def key_blocks(js, from_diagonal):
        pairs = range(n_pairs)
        blocks = range(len(js))
        ks = [pl.multiple_of(j * Q_BLOCK, Q_BLOCK) for j in js]
        both = [(b, p) for b in blocks for p in pairs]
        z, neg_log1m, log_sig, sums, att = {}, {}, {}, {}, {}
        for b, p in both:
            z[b, p] = _dot_nt(qs_ref[p], k_ref[pl.ds(ks[b], Q_BLOCK), p * LANES:(p + 1) * LANES])
        for b, p in both:
            t = jnp.maximum(z[b, p], 0.0) + jnp.log(1.0 + jnp.exp(-jnp.abs(z[b, p])))
            log_sig[b, p] = z[b, p] - t
            masked = from_diagonal and b == 0
            neg_log1m[b, p] = jnp.where(strictly_before, t, 0.0) if masked else t
        for b, p in both:
            hi, lo = _two_bf16_terms(neg_log1m[b, p])
            sums[b, p] = _dot(jnp.concatenate([hi, lo], axis=1), suffix_and_total)
        tails, totals = {}, []
        for p in pairs:
            later = None if from_diagonal else later_ref[p]
            for b in blocks:
                tail = sums[b, p][:, :Q_BLOCK]
                total = sums[b, p][:, Q_BLOCK:]
                if later is not None:
                    tail = tail + later
                    total = total + later
                tails[b, p] = tail
                later = total
            later_ref[p] = later
            totals.append(later)
        least = jnp.min(jnp.minimum(jnp.minimum(totals[0], totals[1]), jnp.minimum(totals[2], totals[3])))
        for b, p in both:
            a = jnp.exp(log_sig[b, p] - tails[b, p])
            if from_diagonal and b == 0:
                a = jnp.where(strictly_before, a, 0.0)
            a = a.astype(BF16)
            att[b, p] = jnp.concatenate([a[:Q_BLOCK], a[Q_BLOCK:]], axis=1)
        for p in pairs:
            pv = None if from_diagonal else acc_ref[p]
            for b in blocks:
                term = _dot(att[b, p], vs_ref[js[b], p])
                pv = term if pv is None else pv + term
            acc_ref[p] = pv
        return least

    least = lax.cond(i > 0, lambda: key_blocks([i, i - 1], True), lambda: key_blocks([i], True))

    def more(carry):
        jj, least = carry
        return jnp.logical_and(jj < i, least < EXP_UNDERFLOW)

    def earlier(carry):
        jj, _ = carry
        return jj + 1, key_blocks([i - 1 - jj], False)

    lax.while_loop(more, earlier, (jnp.int32(1), least))

    for p in range(n_pairs):
        o_ref[:, p * LANES:(p + 1) * LANES] = acc_ref[p].astype(o_ref.dtype)


def _stick(pj):
    b, s, _ = pj.shape
    w = BRANCH_WIDTH
    return pl.pallas_call(
        _stick_kernel,
        grid=(b, s // Q_BLOCK),
        in_specs=[
            pl.BlockSpec((None, Q_BLOCK, w), lambda bi, i: (bi, i, QKV_D_Q)),
            pl.BlockSpec((None, s, w), lambda bi, i: (bi, 0, QKV_D_K)),
            pl.BlockSpec((None, s, w), lambda bi, i: (bi, 0, QKV_D_V)),
        ],
        out_specs=pl.BlockSpec((None, Q_BLOCK, w), lambda bi, i: (bi, i, 0)),
        out_shape=jax.ShapeDtypeStruct((b, s, w), BF16),
        scratch_shapes=[
            pltpu.VMEM((D_HEADS // 2, 2 * Q_BLOCK, LANES), BF16),
            pltpu.VMEM((s // Q_BLOCK, D_HEADS // 2, 2 * Q_BLOCK, LANES), BF16),
            pltpu.VMEM((D_HEADS // 2, 2 * Q_BLOCK, LANES), F32),
            pltpu.VMEM((D_HEADS // 2, Q_BLOCK, LANES), F32),
        ],
        compiler_params=_params(("parallel", "arbitrary"), 2 * Q_BLOCK * w * 2 + 2 * s * w * 2,
                                2 * s * w * 2 + 8 * 1024 * 1024),
        name="mixer_d_stick",
    )(pj, pj, pj)


def _merge_kernel(ya_ref, yb_ref, yc_ref, yd_ref, xn_ref, h_ref, wgate_ref, wb_ref, wo_ref, o_ref):
    d = h_ref.shape[1]
    xn = xn_ref[...]
    mixed = None
    for n, y_ref in enumerate((ya_ref, yb_ref, yc_ref, yd_ref)):
        gate = _sigmoid(_dot(xn, wgate_ref[:, n * d:(n + 1) * d]))
        term = gate * _dot(y_ref[...], wb_ref[n])
        mixed = term if mixed is None else mixed + term
    o_ref[...] = h_ref[...] + _dot(mixed.astype(BF16), wo_ref[...])


def _merge(ya, yb, yc, yd, xn, h, wgate, wb, wo, layer, tm):
    t, d = h.shape
    w = BRANCH_WIDTH
    y_spec = pl.BlockSpec((tm, w), lambda i: (i, 0))
    row = pl.BlockSpec((tm, d), lambda i: (i, 0))
    resident = functools.partial(pl.BlockSpec, pipeline_mode=pl.Buffered(1))
    return pl.pallas_call(
        _merge_kernel,
        grid=(t // tm,),
        in_specs=[
            y_spec, y_spec, y_spec, y_spec, row, row,
            resident((None, d, N_BRANCH * d), lambda i: (layer, 0, 0)),
            resident((None, N_BRANCH, w, d), lambda i: (layer, 0, 0, 0)),
            resident((None, d, d), lambda i: (layer, 0, 0)),
        ],
        out_specs=row,
        out_shape=jax.ShapeDtypeStruct((t, d), F32),
        compiler_params=_params(
            ("parallel",),
            4 * tm * w * 2 + tm * d * 2 + 2 * tm * d * 4,
            (N_BRANCH * d * d + N_BRANCH * w * d + d * d) * 2 + 6 * tm * d * 4),
        name="merge",
    )(ya, yb, yc, yd, xn, h, wgate, wb, wo)


def _ffn_kernel(h_ref, g_ref, wg_ref, wu_ref, wd_ref, gnext_ref, *out_refs, hidden_chunk, last):
    h = h_ref[...]
    hn = _rmsnorm(h, g_ref[...]).astype(BF16)
    hidden = wg_ref.shape[1]
    acc = h
    for c0 in range(0, hidden, hidden_chunk):
        cs = slice(c0, min(c0 + hidden_chunk, hidden))
        gate = _dot(hn, wg_ref[:, cs])
        up = _dot(hn, wu_ref[:, cs])
        act = (gate * _sigmoid(gate) * up).astype(BF16)
        acc = acc + _dot(act, wd_ref[cs, :])
    normed = _rmsnorm(acc, gnext_ref[...])
    if last:
        out_refs[0][...] = normed
    else:
        out_refs[0][...] = acc
        out_refs[1][...] = normed.astype(BF16)


def _ffn(h, g, wg, wu, wd, gnext, layer, next_layer, tm):
    t, d = h.shape
    hidden = wg.shape[-1]
    last = next_layer is None
    row = pl.BlockSpec((tm, d), lambda i: (i, 0))
    resident = functools.partial(pl.BlockSpec, pipeline_mode=pl.Buffered(1))
    if last:
        out_specs, out_shape = row, jax.ShapeDtypeStruct((t, d), F32)
    else:
        out_specs = (row, row)
        out_shape = (jax.ShapeDtypeStruct((t, d), F32), jax.ShapeDtypeStruct((t, d), BF16))
    gnext_row = 0 if last else next_layer
    return pl.pallas_call(
        functools.partial(_ffn_kernel, hidden_chunk=512, last=last),
        grid=(t // tm,),
        in_specs=[
            row,
            pl.BlockSpec((None, 1, d), lambda i: (layer, 0, 0)),
            resident((None, d, hidden), lambda i: (layer, 0, 0)),
            resident((None, d, hidden), lambda i: (layer, 0, 0)),
            resident((None, hidden, d), lambda i: (layer, 0, 0)),
            pl.BlockSpec((None, 1, d), lambda i: (gnext_row, 0, 0)),
        ],
        out_specs=out_specs,
        out_shape=out_shape,
        compiler_params=_params(("parallel",), 3 * tm * d * 4,
                                3 * d * hidden * 2 + 3 * tm * d * 4 + 3 * tm * 512 * 4),
        name="ffn",
    )(h, g, wg, wu, wd, gnext)


def kernel(x, norm_mix, w_in, a_w_gk, a_b_gk, a_norm, b_rel_bias, c_conv_w, c_conv_b, c_w_a, c_b_a,
           c_w_x, c_b_x, c_lambda, w_branch, w_out, norm_ffn, w_ffn_gate, w_ffn_up, w_ffn_down,
           norm_final):
    bsz, s, d = x.shape
    depth = w_in.shape[0]
    t = bsz * s
    w = BRANCH_WIDTH

    sizes = (A_HEADS * HEAD_DIM, A_HEADS * HEAD_DIM, A_HEADS * A_DV, A_RANK, A_HEADS * A_DV,
             w, w, w, w, w, w, w, w, N_BRANCH * d)
    offs = [0]
    for sz in sizes:
        offs.append(offs[-1] + sz)
    (o_aq, o_ak, o_av, o_ar, o_ag, o_bq, _, _, o_cg, _, o_dq, _, _, o_gate, o_end) = offs

    w_in_b = w_in.astype(BF16)
    w_qkv = jnp.concatenate(
        [w_in_b[:, :, o_av:o_ar], w_in_b[:, :, o_bq:o_cg], w_in_b[:, :, o_dq:o_gate]], axis=-1)
    w_f = jnp.concatenate(
        [w_in_b[:, :, o_ag:o_bq], w_in_b[:, :, o_cg:o_dq], w_in_b[:, :, o_aq:o_av],
         w_in_b[:, :, o_ar:o_ag], jnp.zeros((depth, d, 2 * LANES - A_RANK), BF16)], axis=-1)
    w_gate = w_in_b[:, :, o_gate:o_end]
    wgk = jnp.pad(a_w_gk, ((0, 0), (0, LANES - A_RANK), (0, 0))).astype(BF16)
    vec = lambda p: p.reshape(depth, 1, -1)
    bias = _band_bias(b_rel_bias)
    wa = _block_diag(c_w_a).astype(BF16)
    wx = _block_diag(c_w_x).astype(BF16)
    wb = w_branch.astype(BF16)
    wo = w_out.astype(BF16)
    wg = w_ffn_gate.astype(BF16)
    wu = w_ffn_up.astype(BF16)
    wd = w_ffn_down.astype(BF16)
    gf = norm_final.reshape(1, 1, d)

    h = x.reshape(t, d)
    tm_proj = min(1024, t)
    xn = _norm(h, vec(norm_mix), 0, tm_proj)
    for layer in range(depth):
        pj = _proj(xn, w_qkv, layer, BF16, tm_proj, QKV_COLS // 2).reshape(bsz, s, QKV_COLS)
        pf = _proj(xn, w_f, layer, F32, tm_proj, F_COLS // 2).reshape(bsz, s, F_COLS)
        ya = _gla(pf, pj, wgk, vec(a_b_gk), vec(a_norm), layer, 512)
        yb = _chunk_attn(pj, bias, layer)
        yc = _rglru(pf, c_conv_w, vec(c_conv_b), wa, vec(c_b_a), wx, vec(c_b_x), vec(c_lambda),
                    layer, 512)
        yd = _stick(pj)
        h = _merge(ya.reshape(t, w), yb.reshape(t, w), yc.reshape(t, w), yd.reshape(t, w),
                   xn, h, w_gate, wb, wo, layer, 512)
        if layer + 1 < depth:
            h, xn = _ffn(h, vec(norm_ffn), wg, wu, wd, vec(norm_mix), layer, layer + 1, 512)
        else:
            h = _ffn(h, vec(norm_ffn), wg, wu, wd, gf, layer, None, 512)
    return h.reshape(bsz, s, d)
```

```python
import functools

import jax
import jax.numpy as jnp
from jax import lax
from jax.experimental import pallas as pl
from jax.experimental.pallas import tpu as pltpu

F32 = jnp.float32
BF16 = jnp.bfloat16

EPS = 1e-6
CHUNK = 64
BRANCH_WIDTH = 512
N_BRANCH = 4
HEAD_DIM = 64
A_HEADS = 4
A_DV = 128
A_RANK = 16
A_GATE_NORM = 16.0
B_HEADS = 8
B_PREV_CHUNKS = 8
B_MAX_REL = 128
C_BLOCKS = 8
C_CONV = 4
C_POW = 8.0
D_HEADS = 8
MASK_VALUE = -1e30
EXP_UNDERFLOW = 104.0
ALWAYS_VISITED = 3

LANES = 128
SUBLANES = 8
VMEM_BYTES_V7X = 64 * 1024 * 1024

QKV_COLS = 7 * BRANCH_WIDTH
QKV_A_V, QKV_B_Q, QKV_B_K, QKV_B_V, QKV_D_Q, QKV_D_K, QKV_D_V = range(7)
F_A_G, F_C_G, F_C_X, F_A_QK = range(4)
F_A_R_COL = 4 * BRANCH_WIDTH
F_COLS = F_A_R_COL + 2 * LANES

Q_BLOCK = 128
BAND = (B_PREV_CHUNKS + 2) * CHUNK
B_PAD = B_PREV_CHUNKS * CHUNK


def _dot(a, b):
    return jnp.dot(a, b, preferred_element_type=F32)


def _dot_nt(a, b):
    return lax.dot_general(a, b, (((1,), (1,)), ((), ())), preferred_element_type=F32)


def _dot_tn(a, b):
    return lax.dot_general(a, b, (((0,), (0,)), ((), ())), preferred_element_type=F32)


def _two_bf16_terms(x):
    hi = x.astype(BF16)
    lo = (x - hi.astype(F32)).astype(BF16)
    return hi, lo


def _log1p_exp_neg_abs(z):
    return jnp.log1p(jnp.exp(-jnp.abs(z)))


def _log_sigmoid(z):
    return jnp.minimum(z, 0.0) - _log1p_exp_neg_abs(z)


def _sigmoid(z):
    return 1.0 / (1.0 + jnp.exp(-z))


def _vmem_limit(pipelined_bytes, resident_bytes):
    want = 2 * pipelined_bytes + resident_bytes
    return int(min(max(want, 16 * 1024 * 1024), VMEM_BYTES_V7X - 8 * 1024 * 1024))


def _params(semantics, pipelined_bytes, resident_bytes):
    return pltpu.CompilerParams(
        dimension_semantics=semantics,
        vmem_limit_bytes=_vmem_limit(pipelined_bytes, resident_bytes))


def _rmsnorm(x, g):
    return x * lax.rsqrt(jnp.mean(x * x, axis=-1, keepdims=True) + EPS) * g


def _norm_kernel(x_ref, g_ref, o_ref):
    o_ref[...] = _rmsnorm(x_ref[...], g_ref[...]).astype(o_ref.dtype)


def _norm(x, g, layer, tm):
    t, d = x.shape
    return pl.pallas_call(
        _norm_kernel,
        grid=(t // tm,),
        in_specs=[
            pl.BlockSpec((tm, d), lambda i: (i, 0)),
            pl.BlockSpec((None, 1, d), lambda i: (layer, 0, 0)),
        ],
        out_specs=pl.BlockSpec((tm, d), lambda i: (i, 0)),
        out_shape=jax.ShapeDtypeStruct((t, d), BF16),
        compiler_params=_params(("parallel",), tm * d * 6, 2 * tm * d * 4),
        name="norm",
    )(x, g)


def _proj_kernel(x_ref, w_ref, o_ref):
    o_ref[...] = _dot(x_ref[...], w_ref[...]).astype(o_ref.dtype)


def _proj(xn, w, layer, out_dtype, tm, tn):
    t, d = xn.shape
    n = w.shape[-1]
    out_bytes = jnp.dtype(out_dtype).itemsize
    return pl.pallas_call(
        _proj_kernel,
        grid=(t // tm, n // tn),
        in_specs=[
            pl.BlockSpec((tm, d), lambda i, j: (i, 0)),
            pl.BlockSpec((None, d, tn), lambda i, j: (layer, 0, j)),
        ],
        out_specs=pl.BlockSpec((tm, tn), lambda i, j: (i, j)),
        out_shape=jax.ShapeDtypeStruct((t, n), out_dtype),
        compiler_params=_params(
            ("parallel", "arbitrary"),
            tm * d * 2 + d * tn * 2 + tm * tn * out_bytes,
            tm * tn * 4),
        name="proj",
    )(xn, w)


def _gla_kernel(qk_ref, v_ref, g_ref, r_ref, wgk_ref, bgk_ref, ng_ref, o_ref, state_ref, *,
                n_chunks):
    dk_all = A_HEADS * HEAD_DIM
    chunks = range(n_chunks)
    rows = [slice(c * CHUNK, (c + 1) * CHUNK) for c in chunks]

    @pl.when(pl.program_id(1) == 0)
    def _():
        state_ref[...] = jnp.zeros_like(state_ref)

    tri2 = jnp.where(lax.broadcasted_iota(jnp.int32, (CHUNK, 2 * CHUNK), 0)
                     >= lax.broadcasted_iota(jnp.int32, (CHUNK, 2 * CHUNK), 1) % CHUNK,
                     1.0, 0.0).astype(BF16)
    own_head = (lax.broadcasted_iota(jnp.int32, (dk_all, dk_all), 0) // HEAD_DIM
                == lax.broadcasted_iota(jnp.int32, (dk_all, dk_all), 1) // HEAD_DIM)

    def head_stacked(x):
        return jnp.where(own_head, jnp.concatenate([x] * A_HEADS, axis=0), 0.0).astype(BF16)

    r = r_ref[...].astype(BF16)
    gk = _log_sigmoid(_dot(r, wgk_ref[...]) + bgk_ref[...]) * (1.0 / A_GATE_NORM)
    gk_hi, gk_lo = _two_bf16_terms(gk)
    k = qk_ref[:, dk_all:]
    q = qk_ref[:, :dk_all] * (HEAD_DIM ** -0.5)

    k_dec, q_st, v_st, decay = [], [], [], []
    for c in chunks:
        cum = _dot(tri2, jnp.concatenate([gk_hi[rows[c]], gk_lo[rows[c]]], axis=0))
        tot = cum[CHUNK - 1:CHUNK, :]
        k_dec.append(head_stacked(k[rows[c]] * jnp.exp(tot - cum)))
        q_st.append(head_stacked(q[rows[c]]))
        v_c = v_ref[rows[c], :]
        v_st.append(jnp.concatenate([v_c[:, h * A_DV:(h + 1) * A_DV] for h in range(A_HEADS)], axis=0))
        decay.append(jnp.exp(tot))
    kv = [_dot_tn(v_st[c], k_dec[c]) for c in chunks]

    state = state_ref[...]
    outs = []
    for c in chunks:
        state = state * decay[c] + kv[c]
        outs.append(_dot_nt(q_st[c], state.astype(BF16)))
    state_ref[...] = state

    for h in range(A_HEADS):
        cols = slice(h * A_DV, (h + 1) * A_DV)
        o = jnp.concatenate([outs[c][h * CHUNK:(h + 1) * CHUNK] for c in chunks], axis=0)
        o = o * lax.rsqrt(jnp.mean(o * o, axis=-1, keepdims=True) + EPS) * ng_ref[...]
        g = g_ref[:, cols]
        o_ref[:, cols] = (o * (g * _sigmoid(g))).astype(o_ref.dtype)


def _gla(pf, pj, wgk, bgk, ng, layer, tq):
    b, s, _ = pf.shape
    w = BRANCH_WIDTH
    return pl.pallas_call(
        functools.partial(_gla_kernel, n_chunks=tq // CHUNK),
        grid=(b, s // tq),
        in_specs=[
            pl.BlockSpec((None, tq, w), lambda bi, i: (bi, i, F_A_QK)),
            pl.BlockSpec((None, tq, w), lambda bi, i: (bi, i, QKV_A_V)),
            pl.BlockSpec((None, tq, w), lambda bi, i: (bi, i, F_A_G)),
            pl.BlockSpec((None, tq, LANES), lambda bi, i: (bi, i, F_A_R_COL // LANES)),
            pl.BlockSpec((None, LANES, A_HEADS * HEAD_DIM), lambda bi, i: (layer, 0, 0)),
            pl.BlockSpec((None, 1, A_HEADS * HEAD_DIM), lambda bi, i: (layer, 0, 0)),
            pl.BlockSpec((None, 1, A_DV), lambda bi, i: (layer, 0, 0)),
        ],
        out_specs=pl.BlockSpec((None, tq, w), lambda bi, i: (bi, i, 0)),
        out_shape=jax.ShapeDtypeStruct((b, s, w), BF16),
        scratch_shapes=[pltpu.VMEM((A_DV, A_HEADS * HEAD_DIM), F32)],
        compiler_params=_params(("parallel", "arbitrary"), tq * w * 12 + tq * LANES * 4,
                                8 * 1024 * 1024),
        name="mixer_a_gla",
    )(pf, pj, pf, pf, wgk, bgk, ng)


def _chunk_attn_kernel(q_ref, k_ref, v_ref, bias_ref, o_ref, qs_ref, kpad_ref, vfirst_ref,
                       vsecond_ref):
    i = pl.program_id(1)
    s = k_ref.shape[0]
    n_pairs = B_HEADS // 2
    init_rows = 2 * Q_BLOCK

    @pl.when(i == 0)
    def _():
        zeros = jnp.zeros((B_PAD, BRANCH_WIDTH), BF16)
        kpad_ref[0:B_PAD, :] = zeros
        vfirst_ref[0:B_PAD, :] = zeros
        vsecond_ref[0:B_PAD, :] = zeros
        kpad_ref[B_PAD:B_PAD + s, :] = k_ref[...]
        lane = lax.broadcasted_iota(jnp.int32, (init_rows, BRANCH_WIDTH), 1)
        first = (lane % LANES) < HEAD_DIM
        for r0 in range(0, s, init_rows):
            v = v_ref[r0:r0 + init_rows, :].astype(F32)
            vfirst_ref[B_PAD + r0:B_PAD + r0 + init_rows, :] = jnp.where(first, v, 0.0).astype(BF16)
            vsecond_ref[B_PAD + r0:B_PAD + r0 + init_rows, :] = jnp.where(first, 0.0, v).astype(BF16)

    first_head = lax.broadcasted_iota(jnp.int32, (Q_BLOCK, LANES), 1) < HEAD_DIM
    q = q_ref[...].astype(F32) * (HEAD_DIM ** -0.5)
    for p in range(n_pairs):
        q_pair = q[:, p * LANES:(p + 1) * LANES]
        qs_ref[p, 0:Q_BLOCK, :] = jnp.where(first_head, q_pair, 0.0).astype(BF16)
        qs_ref[p, Q_BLOCK:2 * Q_BLOCK, :] = jnp.where(first_head, 0.0, q_pair).astype(BF16)

    start = pl.multiple_of(i * Q_BLOCK, Q_BLOCK)
    window = pl.ds(start, BAND)

    def attend(has_padding):
        pairs = range(n_pairs)
        cols = [slice(p * LANES, (p + 1) * LANES) for p in pairs]
        sc = [_dot_nt(qs_ref[p], kpad_ref[window, cols[p]]) + bias_ref[p] for p in pairs]
        if has_padding:
            in_seq = lax.broadcasted_iota(jnp.int32, (2 * Q_BLOCK, BAND), 1) >= B_PAD - i * Q_BLOCK
            sc = [jnp.where(in_seq, x, MASK_VALUE) for x in sc]
        e = [jnp.exp(x - jnp.max(x, axis=-1, keepdims=True)) for x in sc]
        prob = [(x * (1.0 / jnp.sum(x, axis=-1, keepdims=True))).astype(BF16) for x in e]
        for p in pairs:
            both = jnp.concatenate([prob[p][:Q_BLOCK], prob[p][Q_BLOCK:]], axis=1)
            values = jnp.concatenate([vfirst_ref[window, cols[p]], vsecond_ref[window, cols[p]]], axis=0)
            o_ref[:, cols[p]] = _dot(both, values).astype(o_ref.dtype)

    @pl.when(i < B_PAD // Q_BLOCK)
    def _():
        attend(True)

    @pl.when(i >= B_PAD // Q_BLOCK)
    def _():
        attend(False)


def _chunk_attn(pj, bias, layer):
    b, s, _ = pj.shape
    w = BRANCH_WIDTH
    return pl.pallas_call(
        _chunk_attn_kernel,
        grid=(b, s // Q_BLOCK),
        in_specs=[
            pl.BlockSpec((None, Q_BLOCK, w), lambda bi, i: (bi, i, QKV_B_Q)),
            pl.BlockSpec((None, s, w), lambda bi, i: (bi, 0, QKV_B_K)),
            pl.BlockSpec((None, s, w), lambda bi, i: (bi, 0, QKV_B_V)),
            pl.BlockSpec((None, B_HEADS // 2, 2 * Q_BLOCK, BAND), lambda bi, i: (layer, 0, 0, 0)),
        ],
        out_specs=pl.BlockSpec((None, Q_BLOCK, w), lambda bi, i: (bi, i, 0)),
        out_shape=jax.ShapeDtypeStruct((b, s, w), BF16),
        scratch_shapes=[
            pltpu.VMEM((B_HEADS // 2, 2 * Q_BLOCK, LANES), BF16),
            pltpu.VMEM((s + B_PAD, w), BF16),
            pltpu.VMEM((s + B_PAD, w), BF16),
            pltpu.VMEM((s + B_PAD, w), BF16),
        ],
        compiler_params=_params(
            ("parallel", "arbitrary"),
            2 * Q_BLOCK * w * 2 + 2 * s * w * 2 + B_HEADS * Q_BLOCK * BAND * 4,
            3 * (s + B_PAD) * w * 2 + 16 * 1024 * 1024),
        name="mixer_b_chunk_attn",
    )(pj, pj, pj, bias)


def _band_bias(rel_table):
    depth, heads, _ = rel_table.shape
    width = BAND + Q_BLOCK
    ramp_idx = jnp.clip((BAND - 1) - jnp.arange(width), -B_MAX_REL, B_MAX_REL) + B_MAX_REL
    ramp = rel_table.astype(F32)[:, :, ramp_idx]
    flat = jnp.tile(ramp, (1, 1, Q_BLOCK))[:, :, :Q_BLOCK * (width - 1)]
    skew = flat.reshape(depth, heads, Q_BLOCK, width - 1)
    bias = skew[:, :, :, Q_BLOCK - 1:Q_BLOCK - 1 + BAND]
    qi = jnp.arange(Q_BLOCK)[:, None]
    kj = jnp.arange(BAND)[None, :]
    in_band = (kj // CHUNK >= qi // CHUNK) & (kj // CHUNK <= qi // CHUNK + B_PREV_CHUNKS)
    bias = jnp.where(in_band[None, None], bias, MASK_VALUE)
    return bias.reshape(depth, heads // 2, 2 * Q_BLOCK, BAND)


def _rglru_kernel(g_ref, x_ref, cw_ref, cb_ref, wa_ref, ba_ref, wx_ref, bx_ref, lam_ref, o_ref,
                  xe_ref, a_ref, b_ref, h_ref, carry_ref):
    ts = x_ref.shape[0]
    halo = SUBLANES

    @pl.when(pl.program_id(1) == 0)
    def _():
        xe_ref[0:halo, :] = jnp.zeros((halo, BRANCH_WIDTH), F32)
        carry_ref[...] = jnp.zeros_like(carry_ref)

    x = x_ref[...]
    xe_ref[halo:halo + ts, :] = x
    xc = cb_ref[...]
    for j in range(C_CONV - 1):
        lag = C_CONV - 1 - j
        xc = xc + xe_ref[halo - lag:halo - lag + ts, :] * cw_ref[j:j + 1, :]
    xc = xc + x * cw_ref[C_CONV - 1:C_CONV, :]
    xe_ref[0:halo, :] = xe_ref[ts:ts + halo, :]

    xcb = xc.astype(BF16)
    r = _sigmoid(_dot(xcb, wa_ref[...]) + ba_ref[...])
    gate_i = _sigmoid(_dot(xcb, wx_ref[...]) + bx_ref[...])
    lam = lam_ref[...]
    softplus_neg_lam = jnp.maximum(-lam, 0.0) + _log1p_exp_neg_abs(lam)
    log_a = -C_POW * r * softplus_neg_lam
    a = jnp.exp(log_a)
    a_ref[...] = a
    b_ref[...] = jnp.sqrt(-jnp.tanh(log_a) * (a * a + 1.0)) * (gate_i * xc)

    sub = lax.broadcasted_iota(jnp.int32, (SUBLANES, BRANCH_WIDTH), 0)

    def group(gi, carry):
        rows = pl.ds(pl.multiple_of(gi * SUBLANES, SUBLANES), SUBLANES)
        a = a_ref[rows, :]
        bb = b_ref[rows, :]
        for d in (1, 2, 4):
            a_prev = pltpu.roll(a, d, axis=0)
            b_prev = pltpu.roll(bb, d, axis=0)
            has_prev = sub >= d
            bb = jnp.where(has_prev, a * b_prev + bb, bb)
            a = jnp.where(has_prev, a * a_prev, a)
        h = a * carry + bb
        h_ref[rows, :] = h
        return jnp.broadcast_to(h[SUBLANES - 1:SUBLANES, :], (SUBLANES, BRANCH_WIDTH))

    carry_ref[...] = lax.fori_loop(0, ts // SUBLANES, group, carry_ref[...], unroll=8)
    o_ref[...] = (jax.nn.gelu(g_ref[...]) * h_ref[...]).astype(o_ref.dtype)


def _rglru(pf, cw, cb, wa, ba, wx, bx, lam, layer, ts):
    b, s, _ = pf.shape
    w = BRANCH_WIDTH
    vec = pl.BlockSpec((None, 1, w), lambda bi, i: (layer, 0, 0))
    mat = pl.BlockSpec((None, w, w), lambda bi, i: (layer, 0, 0))
    return pl.pallas_call(
        _rglru_kernel,
        grid=(b, s // ts),
        in_specs=[
            pl.BlockSpec((None, ts, w), lambda bi, i: (bi, i, F_C_G)),
            pl.BlockSpec((None, ts, w), lambda bi, i: (bi, i, F_C_X)),
            pl.BlockSpec((None, C_CONV, w), lambda bi, i: (layer, 0, 0)),
            vec, mat, vec, mat, vec, vec,
        ],
        out_specs=pl.BlockSpec((None, ts, w), lambda bi, i: (bi, i, 0)),
        out_shape=jax.ShapeDtypeStruct((b, s, w), BF16),
        scratch_shapes=[
            pltpu.VMEM((ts + 2 * SUBLANES, w), F32),
            pltpu.VMEM((ts, w), F32),
            pltpu.VMEM((ts, w), F32),
            pltpu.VMEM((ts, w), F32),
            pltpu.VMEM((SUBLANES, w), F32),
        ],
        compiler_params=_params(("parallel", "arbitrary"), ts * w * 10 + 2 * w * w * 2,
                                4 * ts * w * 4 + 8 * 1024 * 1024),
        name="mixer_c_rglru",
    )(pf, pf, cw, cb, wa, ba, wx, bx, lam)


def _block_diag(wb):
    l, g, n, _ = wb.shape
    eye = jnp.eye(g, dtype=wb.dtype)
    return (wb[:, :, :, None, :] * eye[None, :, None, :, None]).reshape(l, g * n, g * n)


def _stick_kernel(q_ref, k_ref, v_ref, o_ref, qs_ref, vs_ref, later_ref, acc_ref):
    i = pl.program_id(1)
    n_pairs = D_HEADS // 2
    n_key_blocks = k_ref.shape[0] // Q_BLOCK
    first_head = lax.broadcasted_iota(jnp.int32, (Q_BLOCK, LANES), 1) < HEAD_DIM

    @pl.when(i == 0)
    def _():
        for jb in range(n_key_blocks):
            for p in range(n_pairs):
                v = v_ref[jb * Q_BLOCK:(jb + 1) * Q_BLOCK, p * LANES:(p + 1) * LANES].astype(F32)
                vs_ref[jb, p, 0:Q_BLOCK, :] = jnp.where(first_head, v, 0.0).astype(BF16)
                vs_ref[jb, p, Q_BLOCK:2 * Q_BLOCK, :] = jnp.where(first_head, 0.0, v).astype(BF16)

    q = q_ref[...].astype(F32) * (HEAD_DIM ** -0.5)
    for p in range(n_pairs):
        q_pair = q[:, p * LANES:(p + 1) * LANES]
        qs_ref[p, 0:Q_BLOCK, :] = jnp.where(first_head, q_pair, 0.0).astype(BF16)
        qs_ref[p, Q_BLOCK:2 * Q_BLOCK, :] = jnp.where(first_head, 0.0, q_pair).astype(BF16)

    row = lax.broadcasted_iota(jnp.int32, (2 * Q_BLOCK, Q_BLOCK), 0) % Q_BLOCK
    col = lax.broadcasted_iota(jnp.int32, (2 * Q_BLOCK, Q_BLOCK), 1)
    strictly_before = col < row
    rr = lax.broadcasted_iota(jnp.int32, (2 * Q_BLOCK, 2 * Q_BLOCK), 0) % Q_BLOCK
    cc = lax.broadcasted_iota(jnp.int32, (2 * Q_BLOCK, 2 * Q_BLOCK), 1)
    suffix_and_total = jnp.where((rr > cc) | (cc >= Q_BLOCK), 1.0, 0.0).astype(BF16)

    def key_blocks(js, from_diagonal):
        pairs = range(n_pairs)
        blocks = range(len(js))
        ks = [pl.multiple_of(j * Q_BLOCK, Q_BLOCK) for j in js]
        both = [(b, p) for b in blocks for p in pairs]
        z, neg_log1m, log_sig, sums, att = {}, {}, {}, {}, {}
        for b, p in both:
            z[b, p] = _dot_nt(qs_ref[p], k_ref[pl.ds(ks[b], Q_BLOCK), p * LANES:(p + 1) * LANES])
        for b, p in both:
            t = jnp.maximum(z[b, p], 0.0) + jnp.log(1.0 + jnp.exp(-jnp.abs(z[b, p])))
            log_sig[b, p] = z[b, p] - t
            masked = from_diagonal and b == 0
            neg_log1m[b, p] = jnp.where(strictly_before, t, 0.0) if masked else t
        for b, p in both:
            hi, lo = _two_bf16_terms(neg_log1m[b, p])
            sums[b, p] = _dot(jnp.concatenate([hi, lo], axis=1), suffix_and_total)
        tails, totals = {}, []
        for p in pairs:
            later = None if from_diagonal else later_ref[p]
            for b in blocks:
                tail = sums[b, p][:, :Q_BLOCK]
                total = sums[b, p][:, Q_BLOCK:]
                if later is not None:
                    tail = tail + later
                    total = total + later
                tails[b, p] = tail
                later = total
            later_ref[p] = later
            totals.append(later)
        least = jnp.min(jnp.minimum(jnp.minimum(totals[0], totals[1]), jnp.minimum(totals[2], totals[3])))
        for b, p in both:
            a = jnp.exp(log_sig[b, p] - tails[b, p])
            if from_diagonal and b == 0:
                a = jnp.where(strictly_before, a, 0.0)
            a = a.astype(BF16)
            att[b, p] = jnp.concatenate([a[:Q_BLOCK], a[Q_BLOCK:]], axis=1)
        for p in pairs:
            pv = None if from_diagonal else acc_ref[p]
            for b in blocks:
                term = _dot(att[b, p], vs_ref[js[b], p])
                pv = term if pv is None else pv + term
            acc_ref[p] = pv
        return least

    least = lax.cond(
        i > 1, lambda: key_blocks([i - b for b in range(ALWAYS_VISITED)], True),
        lambda: lax.cond(i > 0, lambda: key_blocks([i, i - 1], True), lambda: key_blocks([i], True)))

    def more(carry):
        jj, least = carry
        return jnp.logical_and(jj < i, least < EXP_UNDERFLOW)

    def earlier(carry):
        jj, _ = carry
        return jj + 1, key_blocks([i - 1 - jj], False)

    lax.while_loop(more, earlier, (jnp.int32(ALWAYS_VISITED - 1), least))

    for p in range(n_pairs):
        o_ref[:, p * LANES:(p + 1) * LANES] = acc_ref[p].astype(o_ref.dtype)


def _stick(pj):
    b, s, _ = pj.shape
    w = BRANCH_WIDTH
    return pl.pallas_call(
        _stick_kernel,
        grid=(b, s // Q_BLOCK),
        in_specs=[
            pl.BlockSpec((None, Q_BLOCK, w), lambda bi, i: (bi, i, QKV_D_Q)),
            pl.BlockSpec((None, s, w), lambda bi, i: (bi, 0, QKV_D_K)),
            pl.BlockSpec((None, s, w), lambda bi, i: (bi, 0, QKV_D_V)),
        ],
        out_specs=pl.BlockSpec((None, Q_BLOCK, w), lambda bi, i: (bi, i, 0)),
        out_shape=jax.ShapeDtypeStruct((b, s, w), BF16),
        scratch_shapes=[
            pltpu.VMEM((D_HEADS // 2, 2 * Q_BLOCK, LANES), BF16),
            pltpu.VMEM((s // Q_BLOCK, D_HEADS // 2, 2 * Q_BLOCK, LANES), BF16),
            pltpu.VMEM((D_HEADS // 2, 2 * Q_BLOCK, LANES), F32),
            pltpu.VMEM((D_HEADS // 2, Q_BLOCK, LANES), F32),
        ],
        compiler_params=_params(("parallel", "arbitrary"), 2 * Q_BLOCK * w * 2 + 2 * s * w * 2,
                                2 * s * w * 2 + 8 * 1024 * 1024),
        name="mixer_d_stick",
    )(pj, pj, pj)


def _merge_kernel(ya_ref, yb_ref, yc_ref, yd_ref, xn_ref, h_ref, wgate_ref, wb_ref, wo_ref, o_ref):
    d = h_ref.shape[1]
    xn = xn_ref[...]
    mixed = None
    for n, y_ref in enumerate((ya_ref, yb_ref, yc_ref, yd_ref)):
        gate = _sigmoid(_dot(xn, wgate_ref[:, n * d:(n + 1) * d]))
        term = gate * _dot(y_ref[...], wb_ref[n])
        mixed = term if mixed is None else mixed + term
    o_ref[...] = h_ref[...] + _dot(mixed.astype(BF16), wo_ref[...])


def _merge(ya, yb, yc, yd, xn, h, wgate, wb, wo, layer, tm):
    t, d = h.shape
    w = BRANCH_WIDTH
    y_spec = pl.BlockSpec((tm, w), lambda i: (i, 0))
    row = pl.BlockSpec((tm, d), lambda i: (i, 0))
    resident = functools.partial(pl.BlockSpec, pipeline_mode=pl.Buffered(1))
    return pl.pallas_call(
        _merge_kernel,
        grid=(t // tm,),
        in_specs=[
            y_spec, y_spec, y_spec, y_spec, row, row,
            resident((None, d, N_BRANCH * d), lambda i: (layer, 0, 0)),
            resident((None, N_BRANCH, w, d), lambda i: (layer, 0, 0, 0)),
            resident((None, d, d), lambda i: (layer, 0, 0)),
        ],
        out_specs=row,
        out_shape=jax.ShapeDtypeStruct((t, d), F32),
        compiler_params=_params(
            ("parallel",),
            4 * tm * w * 2 + tm * d * 2 + 2 * tm * d * 4,
            (N_BRANCH * d * d + N_BRANCH * w * d + d * d) * 2 + 6 * tm * d * 4),
        name="merge",
    )(ya, yb, yc, yd, xn, h, wgate, wb, wo)


def _ffn_kernel(h_ref, g_ref, wg_ref, wu_ref, wd_ref, gnext_ref, *out_refs, hidden_chunk, last):
    h = h_ref[...]
    hn = _rmsnorm(h, g_ref[...]).astype(BF16)
    hidden = wg_ref.shape[1]
    acc = h
    for c0 in range(0, hidden, hidden_chunk):
        cs = slice(c0, min(c0 + hidden_chunk, hidden))
        gate = _dot(hn, wg_ref[:, cs])
        up = _dot(hn, wu_ref[:, cs])
        act = (gate * _sigmoid(gate) * up).astype(BF16)
        acc = acc + _dot(act, wd_ref[cs, :])
    normed = _rmsnorm(acc, gnext_ref[...])
    if last:
        out_refs[0][...] = normed
    else:
        out_refs[0][...] = acc
        out_refs[1][...] = normed.astype(BF16)


def _ffn(h, g, wg, wu, wd, gnext, layer, next_layer, tm):
    t, d = h.shape
    hidden = wg.shape[-1]
    last = next_layer is None
    row = pl.BlockSpec((tm, d), lambda i: (i, 0))
    resident = functools.partial(pl.BlockSpec, pipeline_mode=pl.Buffered(1))
    if last:
        out_specs, out_shape = row, jax.ShapeDtypeStruct((t, d), F32)
    else:
        out_specs = (row, row)
        out_shape = (jax.ShapeDtypeStruct((t, d), F32), jax.ShapeDtypeStruct((t, d), BF16))
    gnext_row = 0 if last else next_layer
    return pl.pallas_call(
        functools.partial(_ffn_kernel, hidden_chunk=512, last=last),
        grid=(t // tm,),
        in_specs=[
            row,
            pl.BlockSpec((None, 1, d), lambda i: (layer, 0, 0)),
            resident((None, d, hidden), lambda i: (layer, 0, 0)),
            resident((None, d, hidden), lambda i: (layer, 0, 0)),
            resident((None, hidden, d), lambda i: (layer, 0, 0)),
            pl.BlockSpec((None, 1, d), lambda i: (gnext_row, 0, 0)),
        ],
        out_specs=out_specs,
        out_shape=out_shape,
        compiler_params=_params(("parallel",), 3 * tm * d * 4,
                                3 * d * hidden * 2 + 3 * tm * d * 4 + 3 * tm * 512 * 4),
        name="ffn",
    )(h, g, wg, wu, wd, gnext)


def kernel(x, norm_mix, w_in, a_w_gk, a_b_gk, a_norm, b_rel_bias, c_conv_w, c_conv_b, c_w_a, c_b_a,
           c_w_x, c_b_x, c_lambda, w_branch, w_out, norm_ffn, w_ffn_gate, w_ffn_up, w_ffn_down,
           norm_final):
    bsz, s, d = x.shape
    depth = w_in.shape[0]
    t = bsz * s
    w = BRANCH_WIDTH

    sizes = (A_HEADS * HEAD_DIM, A_HEADS * HEAD_DIM, A_HEADS * A_DV, A_RANK, A_HEADS * A_DV,
             w, w, w, w, w, w, w, w, N_BRANCH * d)
    offs = [0]
    for sz in sizes:
        offs.append(offs[-1] + sz)
    (o_aq, o_ak, o_av, o_ar, o_ag, o_bq, _, _, o_cg, _, o_dq, _, _, o_gate, o_end) = offs

    w_in_b = w_in.astype(BF16)
    w_qkv = jnp.concatenate(
        [w_in_b[:, :, o_av:o_ar], w_in_b[:, :, o_bq:o_cg], w_in_b[:, :, o_dq:o_gate]], axis=-1)
    w_f = jnp.concatenate(
        [w_in_b[:, :, o_ag:o_bq], w_in_b[:, :, o_cg:o_dq], w_in_b[:, :, o_aq:o_av],
         w_in_b[:, :, o_ar:o_ag], jnp.zeros((depth, d, 2 * LANES - A_RANK), BF16)], axis=-1)
    w_gate = w_in_b[:, :, o_gate:o_end]
    wgk = jnp.pad(a_w_gk, ((0, 0), (0, LANES - A_RANK), (0, 0))).astype(BF16)
    vec = lambda p: p.reshape(depth, 1, -1)
    bias = _band_bias(b_rel_bias)
    wa = _block_diag(c_w_a).astype(BF16)
    wx = _block_diag(c_w_x).astype(BF16)
    wb = w_branch.astype(BF16)
    wo = w_out.astype(BF16)
    wg = w_ffn_gate.astype(BF16)
    wu = w_ffn_up.astype(BF16)
    wd = w_ffn_down.astype(BF16)
    gf = norm_final.reshape(1, 1, d)

    h = x.reshape(t, d)
    tm_proj = min(1024, t)
    xn = _norm(h, vec(norm_mix), 0, tm_proj)
    for layer in range(depth):
        pj = _proj(xn, w_qkv, layer, BF16, tm_proj, QKV_COLS // 2).reshape(bsz, s, QKV_COLS)
        pf = _proj(xn, w_f, layer, F32, tm_proj, F_COLS // 2).reshape(bsz, s, F_COLS)
        ya = _gla(pf, pj, wgk, vec(a_b_gk), vec(a_norm), layer, 512)
        yb = _chunk_attn(pj, bias, layer)
        yc = _rglru(pf, c_conv_w, vec(c_conv_b), wa, vec(c_b_a), wx, vec(c_b_x), vec(c_lambda),
                    layer, 512)
        yd = _stick(pj)
        h = _merge(ya.reshape(t, w), yb.reshape(t, w), yc.reshape(t, w), yd.reshape(t, w),
                   xn, h, w_gate, wb, wo, layer, 512)
        if layer + 1 < depth:
            h, xn = _ffn(h, vec(norm_ffn), wg, wu, wd, vec(norm_mix), layer, layer + 1, 512)
        else:
            h = _ffn(h, vec(norm_ffn), wg, wu, wd, gf, layer, None, 512)
    return h.reshape(bsz, s, d)
```

```python
import functools

import jax
import jax.numpy as jnp
from jax import lax
from jax.experimental import pallas as pl
from jax.experimental.pallas import tpu as pltpu

F32 = jnp.float32
BF16 = jnp.bfloat16

EPS = 1e-6
CHUNK = 64
BRANCH_WIDTH = 512
N_BRANCH = 4
HEAD_DIM = 64
A_HEADS = 4
A_DV = 128
A_RANK = 16
A_GATE_NORM = 16.0
B_HEADS = 8
B_PREV_CHUNKS = 8
B_MAX_REL = 128
C_BLOCKS = 8
C_CONV = 4
C_POW = 8.0
D_HEADS = 8
MASK_VALUE = -1e30
EXP_UNDERFLOW = 104.0
ALWAYS_VISITED = 3

LANES = 128
SUBLANES = 8
VMEM_BYTES_V7X = 64 * 1024 * 1024

QKV_COLS = 7 * BRANCH_WIDTH
QKV_A_V, QKV_B_Q, QKV_B_K, QKV_B_V, QKV_D_Q, QKV_D_K, QKV_D_V = range(7)
F_A_G, F_C_G, F_C_X, F_A_QK = range(4)
F_A_R_COL = 4 * BRANCH_WIDTH
F_COLS = F_A_R_COL + 2 * LANES

Q_BLOCK = 128
Q_BLOCKS_PER_STEP = 2
BAND = (B_PREV_CHUNKS + 2) * CHUNK
B_PAD = B_PREV_CHUNKS * CHUNK


def _dot(a, b):
    return jnp.dot(a, b, preferred_element_type=F32)


def _dot_nt(a, b):
    return lax.dot_general(a, b, (((1,), (1,)), ((), ())), preferred_element_type=F32)


def _dot_tn(a, b):
    return lax.dot_general(a, b, (((0,), (0,)), ((), ())), preferred_element_type=F32)


def _two_bf16_terms(x):
    hi = x.astype(BF16)
    lo = (x - hi.astype(F32)).astype(BF16)
    return hi, lo


def _log1p_exp_neg_abs(z):
    return jnp.log1p(jnp.exp(-jnp.abs(z)))


def _log_sigmoid(z):
    return jnp.minimum(z, 0.0) - _log1p_exp_neg_abs(z)


def _sigmoid(z):
    return 1.0 / (1.0 + jnp.exp(-z))


VMEM_REQUEST_CAP = VMEM_BYTES_V7X - 8 * 1024 * 1024


def _vmem_limit(pipelined_bytes, resident_bytes):
    want = 2 * pipelined_bytes + resident_bytes
    return int(min(max(want, 16 * 1024 * 1024), VMEM_REQUEST_CAP))


def _params(semantics, pipelined_bytes, resident_bytes, result_in_hbm=False):
    limit = VMEM_REQUEST_CAP if result_in_hbm else _vmem_limit(pipelined_bytes, resident_bytes)
    return pltpu.CompilerParams(dimension_semantics=semantics, vmem_limit_bytes=limit)


def _rmsnorm(x, g):
    return x * lax.rsqrt(jnp.mean(x * x, axis=-1, keepdims=True) + EPS) * g


def _norm_kernel(x_ref, g_ref, o_ref):
    o_ref[...] = _rmsnorm(x_ref[...], g_ref[...]).astype(o_ref.dtype)


def _norm(x, g, layer, tm):
    t, d = x.shape
    return pl.pallas_call(
        _norm_kernel,
        grid=(t // tm,),
        in_specs=[
            pl.BlockSpec((tm, d), lambda i: (i, 0)),
            pl.BlockSpec((None, 1, d), lambda i: (layer, 0, 0)),
        ],
        out_specs=pl.BlockSpec((tm, d), lambda i: (i, 0)),
        out_shape=jax.ShapeDtypeStruct((t, d), BF16),
        compiler_params=_params(("parallel",), tm * d * 6, 2 * tm * d * 4, result_in_hbm=True),
        name="norm",
    )(x, g)


def _proj_kernel(x_ref, w_ref, o_ref):
    o_ref[...] = _dot(x_ref[...], w_ref[...]).astype(o_ref.dtype)


def _proj(xn, w, layer, out_dtype, tm, tn):
    t, d = xn.shape
    n = w.shape[-1]
    out_bytes = jnp.dtype(out_dtype).itemsize
    return pl.pallas_call(
        _proj_kernel,
        grid=(t // tm, n // tn),
        in_specs=[
            pl.BlockSpec((tm, d), lambda i, j: (i, 0)),
            pl.BlockSpec((None, d, tn), lambda i, j: (layer, 0, j)),
        ],
        out_specs=pl.BlockSpec((tm, tn), lambda i, j: (i, j)),
        out_shape=jax.ShapeDtypeStruct((t, n), out_dtype),
        compiler_params=_params(
            ("parallel", "arbitrary"),
            tm * d * 2 + d * tn * 2 + tm * tn * out_bytes,
            tm * tn * 4),
        name="proj",
    )(xn, w)


def _gla_kernel(qk_ref, v_ref, g_ref, r_ref, wgk_ref, bgk_ref, ng_ref, o_ref, state_ref, *,
                n_chunks):
    dk_all = A_HEADS * HEAD_DIM
    chunks = range(n_chunks)
    rows = [slice(c * CHUNK, (c + 1) * CHUNK) for c in chunks]

    @pl.when(pl.program_id(1) == 0)
    def _():
        state_ref[...] = jnp.zeros_like(state_ref)

    tri2 = jnp.where(lax.broadcasted_iota(jnp.int32, (CHUNK, 2 * CHUNK), 0)
                     >= lax.broadcasted_iota(jnp.int32, (CHUNK, 2 * CHUNK), 1) % CHUNK,
                     1.0, 0.0).astype(BF16)
    own_head = (lax.broadcasted_iota(jnp.int32, (dk_all, dk_all), 0) // HEAD_DIM
                == lax.broadcasted_iota(jnp.int32, (dk_all, dk_all), 1) // HEAD_DIM)

    def head_stacked(x):
        return jnp.where(own_head, jnp.concatenate([x] * A_HEADS, axis=0), 0.0).astype(BF16)

    r = r_ref[...].astype(BF16)
    gk = _log_sigmoid(_dot(r, wgk_ref[...]) + bgk_ref[...]) * (1.0 / A_GATE_NORM)
    gk_hi, gk_lo = _two_bf16_terms(gk)
    k = qk_ref[:, dk_all:]
    q = qk_ref[:, :dk_all] * (HEAD_DIM ** -0.5)

    k_dec, q_st, v_st, decay = [], [], [], []
    for c in chunks:
        cum = _dot(tri2, jnp.concatenate([gk_hi[rows[c]], gk_lo[rows[c]]], axis=0))
        tot = cum[CHUNK - 1:CHUNK, :]
        k_dec.append(head_stacked(k[rows[c]] * jnp.exp(tot - cum)))
        q_st.append(head_stacked(q[rows[c]]))
        v_c = v_ref[rows[c], :]
        v_st.append(jnp.concatenate([v_c[:, h * A_DV:(h + 1) * A_DV] for h in range(A_HEADS)], axis=0))
        decay.append(jnp.exp(tot))
    kv = [_dot_tn(v_st[c], k_dec[c]) for c in chunks]

    state = state_ref[...]
    outs = []
    for c in chunks:
        state = state * decay[c] + kv[c]
        outs.append(_dot_nt(q_st[c], state.astype(BF16)))
    state_ref[...] = state

    for h in range(A_HEADS):
        cols = slice(h * A_DV, (h + 1) * A_DV)
        o = jnp.concatenate([outs[c][h * CHUNK:(h + 1) * CHUNK] for c in chunks], axis=0)
        o = o * lax.rsqrt(jnp.mean(o * o, axis=-1, keepdims=True) + EPS) * ng_ref[...]
        g = g_ref[:, cols]
        o_ref[:, cols] = (o * (g * _sigmoid(g))).astype(o_ref.dtype)


def _gla(pf, pj, wgk, bgk, ng, layer, tq):
    b, s, _ = pf.shape
    w = BRANCH_WIDTH
    return pl.pallas_call(
        functools.partial(_gla_kernel, n_chunks=tq // CHUNK),
        grid=(b, s // tq),
        in_specs=[
            pl.BlockSpec((None, tq, w), lambda bi, i: (bi, i, F_A_QK)),
            pl.BlockSpec((None, tq, w), lambda bi, i: (bi, i, QKV_A_V)),
            pl.BlockSpec((None, tq, w), lambda bi, i: (bi, i, F_A_G)),
            pl.BlockSpec((None, tq, LANES), lambda bi, i: (bi, i, F_A_R_COL // LANES)),
            pl.BlockSpec((None, LANES, A_HEADS * HEAD_DIM), lambda bi, i: (layer, 0, 0)),
            pl.BlockSpec((None, 1, A_HEADS * HEAD_DIM), lambda bi, i: (layer, 0, 0)),
            pl.BlockSpec((None, 1, A_DV), lambda bi, i: (layer, 0, 0)),
        ],
        out_specs=pl.BlockSpec((None, tq, w), lambda bi, i: (bi, i, 0)),
        out_shape=jax.ShapeDtypeStruct((b, s, w), BF16),
        scratch_shapes=[pltpu.VMEM((A_DV, A_HEADS * HEAD_DIM), F32)],
        compiler_params=_params(("parallel", "arbitrary"), tq * w * 12 + tq * LANES * 4,
                                8 * 1024 * 1024, result_in_hbm=True),
        name="mixer_a_gla",
    )(pf, pj, pf, pf, wgk, bgk, ng)


def _chunk_attn_kernel(q_ref, k_ref, v_ref, bias_ref, o_ref, qs_ref, kpad_ref, vfirst_ref,
                       vsecond_ref):
    step = pl.program_id(1)
    s = k_ref.shape[0]
    n_pairs = B_HEADS // 2
    init_rows = 2 * Q_BLOCK

    @pl.when(step == 0)
    def _():
        zeros = jnp.zeros((B_PAD, BRANCH_WIDTH), BF16)
        kpad_ref[0:B_PAD, :] = zeros
        vfirst_ref[0:B_PAD, :] = zeros
        vsecond_ref[0:B_PAD, :] = zeros
        kpad_ref[B_PAD:B_PAD + s, :] = k_ref[...]
        lane = lax.broadcasted_iota(jnp.int32, (init_rows, BRANCH_WIDTH), 1)
        first = (lane % LANES) < HEAD_DIM
        for r0 in range(0, s, init_rows):
            v = v_ref[r0:r0 + init_rows, :].astype(F32)
            vfirst_ref[B_PAD + r0:B_PAD + r0 + init_rows, :] = jnp.where(first, v, 0.0).astype(BF16)
            vsecond_ref[B_PAD + r0:B_PAD + r0 + init_rows, :] = jnp.where(first, 0.0, v).astype(BF16)

    first_head = lax.broadcasted_iota(jnp.int32, (Q_BLOCK, LANES), 1) < HEAD_DIM

    def attend(sub, has_padding):
        i = step * Q_BLOCKS_PER_STEP + sub
        rows = slice(sub * Q_BLOCK, (sub + 1) * Q_BLOCK)
        q = q_ref[rows, :].astype(F32) * (HEAD_DIM ** -0.5)
        for p in range(n_pairs):
            q_pair = q[:, p * LANES:(p + 1) * LANES]
            qs_ref[p, 0:Q_BLOCK, :] = jnp.where(first_head, q_pair, 0.0).astype(BF16)
            qs_ref[p, Q_BLOCK:2 * Q_BLOCK, :] = jnp.where(first_head, 0.0, q_pair).astype(BF16)
        window = pl.ds(pl.multiple_of(i * Q_BLOCK, Q_BLOCK), BAND)
        pairs = range(n_pairs)
        cols = [slice(p * LANES, (p + 1) * LANES) for p in pairs]
        sc = [_dot_nt(qs_ref[p], kpad_ref[window, cols[p]]) + bias_ref[p] for p in pairs]
        if has_padding:
            in_seq = lax.broadcasted_iota(jnp.int32, (2 * Q_BLOCK, BAND), 1) >= B_PAD - i * Q_BLOCK
            sc = [jnp.where(in_seq, x, MASK_VALUE) for x in sc]
        e = [jnp.exp(x - jnp.max(x, axis=-1, keepdims=True)) for x in sc]
        prob = [(x * (1.0 / jnp.sum(x, axis=-1, keepdims=True))).astype(BF16) for x in e]
        for p in pairs:
            both = jnp.concatenate([prob[p][:Q_BLOCK], prob[p][Q_BLOCK:]], axis=1)
            values = jnp.concatenate([vfirst_ref[window, cols[p]], vsecond_ref[window, cols[p]]], axis=0)
            o_ref[rows, cols[p]] = _dot(both, values).astype(o_ref.dtype)

    padded_steps = B_PAD // (Q_BLOCK * Q_BLOCKS_PER_STEP)

    @pl.when(step < padded_steps)
    def _():
        for sub in range(Q_BLOCKS_PER_STEP):
            attend(sub, True)

    @pl.when(step >= padded_steps)
    def _():
        for sub in range(Q_BLOCKS_PER_STEP):
            attend(sub, False)


def _chunk_attn(pj, bias, layer):
    b, s, _ = pj.shape
    w = BRANCH_WIDTH
    q_rows = Q_BLOCK * Q_BLOCKS_PER_STEP
    return pl.pallas_call(
        _chunk_attn_kernel,
        grid=(b, s // q_rows),
        in_specs=[
            pl.BlockSpec((None, q_rows, w), lambda bi, i: (bi, i, QKV_B_Q)),
            pl.BlockSpec((None, s, w), lambda bi, i: (bi, 0, QKV_B_K)),
            pl.BlockSpec((None, s, w), lambda bi, i: (bi, 0, QKV_B_V)),
            pl.BlockSpec((None, B_HEADS // 2, 2 * Q_BLOCK, BAND), lambda bi, i: (layer, 0, 0, 0)),
        ],
        out_specs=pl.BlockSpec((None, q_rows, w), lambda bi, i: (bi, i, 0)),
        out_shape=jax.ShapeDtypeStruct((b, s, w), BF16),
        scratch_shapes=[
            pltpu.VMEM((B_HEADS // 2, 2 * Q_BLOCK, LANES), BF16),
            pltpu.VMEM((s + B_PAD, w), BF16),
            pltpu.VMEM((s + B_PAD, w), BF16),
            pltpu.VMEM((s + B_PAD, w), BF16),
        ],
        compiler_params=_params(
            ("parallel", "arbitrary"),
            2 * q_rows * w * 2 + 2 * s * w * 2 + B_HEADS * Q_BLOCK * BAND * 4,
            3 * (s + B_PAD) * w * 2 + 16 * 1024 * 1024),
        name="mixer_b_chunk_attn",
    )(pj, pj, pj, bias)


def _band_bias(rel_table):
    depth, heads, _ = rel_table.shape
    width = BAND + Q_BLOCK
    ramp_idx = jnp.clip((BAND - 1) - jnp.arange(width), -B_MAX_REL, B_MAX_REL) + B_MAX_REL
    ramp = rel_table.astype(F32)[:, :, ramp_idx]
    flat = jnp.tile(ramp, (1, 1, Q_BLOCK))[:, :, :Q_BLOCK * (width - 1)]
    skew = flat.reshape(depth, heads, Q_BLOCK, width - 1)
    bias = skew[:, :, :, Q_BLOCK - 1:Q_BLOCK - 1 + BAND]
    qi = jnp.arange(Q_BLOCK)[:, None]
    kj = jnp.arange(BAND)[None, :]
    in_band = (kj // CHUNK >= qi // CHUNK) & (kj // CHUNK <= qi // CHUNK + B_PREV_CHUNKS)
    bias = jnp.where(in_band[None, None], bias, MASK_VALUE)
    return bias.reshape(depth, heads // 2, 2 * Q_BLOCK, BAND)


def _rglru_kernel(g_ref, x_ref, cw_ref, cb_ref, wa_ref, ba_ref, wx_ref, bx_ref, lam_ref, o_ref,
                  xe_ref, a_ref, b_ref, h_ref, carry_ref):
    ts = x_ref.shape[0]
    halo = SUBLANES

    @pl.when(pl.program_id(1) == 0)
    def _():
        xe_ref[0:halo, :] = jnp.zeros((halo, BRANCH_WIDTH), F32)
        carry_ref[...] = jnp.zeros_like(carry_ref)

    x = x_ref[...]
    xe_ref[halo:halo + ts, :] = x
    xc = cb_ref[...]
    for j in range(C_CONV - 1):
        lag = C_CONV - 1 - j
        xc = xc + xe_ref[halo - lag:halo - lag + ts, :] * cw_ref[j:j + 1, :]
    xc = xc + x * cw_ref[C_CONV - 1:C_CONV, :]
    xe_ref[0:halo, :] = xe_ref[ts:ts + halo, :]

    xcb = xc.astype(BF16)
    r = _sigmoid(_dot(xcb, wa_ref[...]) + ba_ref[...])
    gate_i = _sigmoid(_dot(xcb, wx_ref[...]) + bx_ref[...])
    lam = lam_ref[...]
    softplus_neg_lam = jnp.maximum(-lam, 0.0) + _log1p_exp_neg_abs(lam)
    log_a = -C_POW * r * softplus_neg_lam
    a = jnp.exp(log_a)
    a_ref[...] = a
    b_ref[...] = jnp.sqrt(-jnp.tanh(log_a) * (a * a + 1.0)) * (gate_i * xc)

    sub = lax.broadcasted_iota(jnp.int32, (SUBLANES, BRANCH_WIDTH), 0)

    def group(gi, carry):
        rows = pl.ds(pl.multiple_of(gi * SUBLANES, SUBLANES), SUBLANES)
        a = a_ref[rows, :]
        bb = b_ref[rows, :]
        for d in (1, 2, 4):
            a_prev = pltpu.roll(a, d, axis=0)
            b_prev = pltpu.roll(bb, d, axis=0)
            has_prev = sub >= d
            bb = jnp.where(has_prev, a * b_prev + bb, bb)
            a = jnp.where(has_prev, a * a_prev, a)
        h = a * carry + bb
        h_ref[rows, :] = h
        return jnp.broadcast_to(h[SUBLANES - 1:SUBLANES, :], (SUBLANES, BRANCH_WIDTH))

    carry_ref[...] = lax.fori_loop(0, ts // SUBLANES, group, carry_ref[...], unroll=8)
    o_ref[...] = (jax.nn.gelu(g_ref[...]) * h_ref[...]).astype(o_ref.dtype)


def _rglru(pf, cw, cb, wa, ba, wx, bx, lam, layer, ts):
    b, s, _ = pf.shape
    w = BRANCH_WIDTH
    vec = pl.BlockSpec((None, 1, w), lambda bi, i: (layer, 0, 0))
    mat = pl.BlockSpec((None, w, w), lambda bi, i: (layer, 0, 0))
    return pl.pallas_call(
        _rglru_kernel,
        grid=(b, s // ts),
        in_specs=[
            pl.BlockSpec((None, ts, w), lambda bi, i: (bi, i, F_C_G)),
            pl.BlockSpec((None, ts, w), lambda bi, i: (bi, i, F_C_X)),
            pl.BlockSpec((None, C_CONV, w), lambda bi, i: (layer, 0, 0)),
            vec, mat, vec, mat, vec, vec,
        ],
        out_specs=pl.BlockSpec((None, ts, w), lambda bi, i: (bi, i, 0)),
        out_shape=jax.ShapeDtypeStruct((b, s, w), BF16),
        scratch_shapes=[
            pltpu.VMEM((ts + 2 * SUBLANES, w), F32),
            pltpu.VMEM((ts, w), F32),
            pltpu.VMEM((ts, w), F32),
            pltpu.VMEM((ts, w), F32),
            pltpu.VMEM((SUBLANES, w), F32),
        ],
        compiler_params=_params(("parallel", "arbitrary"), ts * w * 10 + 2 * w * w * 2,
                                4 * ts * w * 4 + 8 * 1024 * 1024, result_in_hbm=True),
        name="mixer_c_rglru",
    )(pf, pf, cw, cb, wa, ba, wx, bx, lam)


def _block_diag(wb):
    l, g, n, _ = wb.shape
    eye = jnp.eye(g, dtype=wb.dtype)
    return (wb[:, :, :, None, :] * eye[None, :, None, :, None]).reshape(l, g * n, g * n)


def _stick_kernel(q_ref, k_ref, v_ref, o_ref, qs_ref, vs_ref, later_ref, acc_ref):
    step = pl.program_id(1)
    n_pairs = D_HEADS // 2
    n_key_blocks = k_ref.shape[0] // Q_BLOCK
    first_head = lax.broadcasted_iota(jnp.int32, (Q_BLOCK, LANES), 1) < HEAD_DIM

    @pl.when(step == 0)
    def _():
        for jb in range(n_key_blocks):
            for p in range(n_pairs):
                v = v_ref[jb * Q_BLOCK:(jb + 1) * Q_BLOCK, p * LANES:(p + 1) * LANES].astype(F32)
                vs_ref[jb, p, 0:Q_BLOCK, :] = jnp.where(first_head, v, 0.0).astype(BF16)
                vs_ref[jb, p, Q_BLOCK:2 * Q_BLOCK, :] = jnp.where(first_head, 0.0, v).astype(BF16)

    row = lax.broadcasted_iota(jnp.int32, (2 * Q_BLOCK, Q_BLOCK), 0) % Q_BLOCK
    col = lax.broadcasted_iota(jnp.int32, (2 * Q_BLOCK, Q_BLOCK), 1)
    strictly_before = col < row
    rr = lax.broadcasted_iota(jnp.int32, (2 * Q_BLOCK, 2 * Q_BLOCK), 0) % Q_BLOCK
    cc = lax.broadcasted_iota(jnp.int32, (2 * Q_BLOCK, 2 * Q_BLOCK), 1)
    suffix_and_total = jnp.where((rr > cc) | (cc >= Q_BLOCK), 1.0, 0.0).astype(BF16)

    def key_blocks(js, from_diagonal):
        pairs = range(n_pairs)
        blocks = range(len(js))
        ks = [pl.multiple_of(j * Q_BLOCK, Q_BLOCK) for j in js]
        both = [(b, p) for b in blocks for p in pairs]
        z, neg_log1m, log_sig, sums, att = {}, {}, {}, {}, {}
        for b, p in both:
            z[b, p] = _dot_nt(qs_ref[p], k_ref[pl.ds(ks[b], Q_BLOCK), p * LANES:(p + 1) * LANES])
        for b, p in both:
            t = jnp.maximum(z[b, p], 0.0) + jnp.log(1.0 + jnp.exp(-jnp.abs(z[b, p])))
            log_sig[b, p] = z[b, p] - t
            masked = from_diagonal and b == 0
            neg_log1m[b, p] = jnp.where(strictly_before, t, 0.0) if masked else t
        for b, p in both:
            hi, lo = _two_bf16_terms(neg_log1m[b, p])
            sums[b, p] = _dot(jnp.concatenate([hi, lo], axis=1), suffix_and_total)
        tails, totals = {}, []
        for p in pairs:
            later = None if from_diagonal else later_ref[p]
            for b in blocks:
                tail = sums[b, p][:, :Q_BLOCK]
                total = sums[b, p][:, Q_BLOCK:]
                if later is not None:
                    tail = tail + later
                    total = total + later
                tails[b, p] = tail
                later = total
            later_ref[p] = later
            totals.append(later)
        least = jnp.min(jnp.minimum(jnp.minimum(totals[0], totals[1]), jnp.minimum(totals[2], totals[3])))
        for b, p in both:
            a = jnp.exp(log_sig[b, p] - tails[b, p])
            if from_diagonal and b == 0:
                a = jnp.where(strictly_before, a, 0.0)
            a = a.astype(BF16)
            att[b, p] = jnp.concatenate([a[:Q_BLOCK], a[Q_BLOCK:]], axis=1)
        for p in pairs:
            pv = None if from_diagonal else acc_ref[p]
            for b in blocks:
                term = _dot(att[b, p], vs_ref[js[b], p])
                pv = term if pv is None else pv + term
            acc_ref[p] = pv
        return least

    def query_block(sub):
        i = step * Q_BLOCKS_PER_STEP + sub
        rows = slice(sub * Q_BLOCK, (sub + 1) * Q_BLOCK)
        q = q_ref[rows, :].astype(F32) * (HEAD_DIM ** -0.5)
        for p in range(n_pairs):
            q_pair = q[:, p * LANES:(p + 1) * LANES]
            qs_ref[p, 0:Q_BLOCK, :] = jnp.where(first_head, q_pair, 0.0).astype(BF16)
            qs_ref[p, Q_BLOCK:2 * Q_BLOCK, :] = jnp.where(first_head, 0.0, q_pair).astype(BF16)

        least = lax.cond(
            i > 1, lambda: key_blocks([i - b for b in range(ALWAYS_VISITED)], True),
            lambda: lax.cond(i > 0, lambda: key_blocks([i, i - 1], True), lambda: key_blocks([i], True)))

        def more(carry):
            jj, least = carry
            return jnp.logical_and(jj < i, least < EXP_UNDERFLOW)

        def earlier(carry):
            jj, _ = carry
            return jj + 1, key_blocks([i - 1 - jj], False)

        lax.while_loop(more, earlier, (jnp.int32(ALWAYS_VISITED - 1), least))

        for p in range(n_pairs):
            o_ref[rows, p * LANES:(p + 1) * LANES] = acc_ref[p].astype(o_ref.dtype)

    for sub in range(Q_BLOCKS_PER_STEP):
        query_block(sub)


def _stick(pj):
    b, s, _ = pj.shape
    w = BRANCH_WIDTH
    q_rows = Q_BLOCK * Q_BLOCKS_PER_STEP
    return pl.pallas_call(
        _stick_kernel,
        grid=(b, s // q_rows),
        in_specs=[
            pl.BlockSpec((None, q_rows, w), lambda bi, i: (bi, i, QKV_D_Q)),
            pl.BlockSpec((None, s, w), lambda bi, i: (bi, 0, QKV_D_K)),
            pl.BlockSpec((None, s, w), lambda bi, i: (bi, 0, QKV_D_V)),
        ],
        out_specs=pl.BlockSpec((None, q_rows, w), lambda bi, i: (bi, i, 0)),
        out_shape=jax.ShapeDtypeStruct((b, s, w), BF16),
        scratch_shapes=[
            pltpu.VMEM((D_HEADS // 2, 2 * Q_BLOCK, LANES), BF16),
            pltpu.VMEM((s // Q_BLOCK, D_HEADS // 2, 2 * Q_BLOCK, LANES), BF16),
            pltpu.VMEM((D_HEADS // 2, 2 * Q_BLOCK, LANES), F32),
            pltpu.VMEM((D_HEADS // 2, Q_BLOCK, LANES), F32),
        ],
        compiler_params=_params(("parallel", "arbitrary"), 2 * q_rows * w * 2 + 2 * s * w * 2,
                                2 * s * w * 2 + 8 * 1024 * 1024, result_in_hbm=True),
        name="mixer_d_stick",
    )(pj, pj, pj)


def _merge_kernel(ya_ref, yb_ref, yc_ref, yd_ref, xn_ref, h_ref, wgate_ref, wb_ref, wo_ref, o_ref):
    d = h_ref.shape[1]
    xn = xn_ref[...]
    mixed = None
    for n, y_ref in enumerate((ya_ref, yb_ref, yc_ref, yd_ref)):
        gate = _sigmoid(_dot(xn, wgate_ref[:, n * d:(n + 1) * d]))
        term = gate * _dot(y_ref[...], wb_ref[n])
        mixed = term if mixed is None else mixed + term
    o_ref[...] = h_ref[...] + _dot(mixed.astype(BF16), wo_ref[...])


def _merge(ya, yb, yc, yd, xn, h, wgate, wb, wo, layer, tm):
    t, d = h.shape
    w = BRANCH_WIDTH
    y_spec = pl.BlockSpec((tm, w), lambda i: (i, 0))
    row = pl.BlockSpec((tm, d), lambda i: (i, 0))
    resident = functools.partial(pl.BlockSpec, pipeline_mode=pl.Buffered(1))
    return pl.pallas_call(
        _merge_kernel,
        grid=(t // tm,),
        in_specs=[
            y_spec, y_spec, y_spec, y_spec, row, row,
            resident((None, d, N_BRANCH * d), lambda i: (layer, 0, 0)),
            resident((None, N_BRANCH, w, d), lambda i: (layer, 0, 0, 0)),
            resident((None, d, d), lambda i: (layer, 0, 0)),
        ],
        out_specs=row,
        out_shape=jax.ShapeDtypeStruct((t, d), F32),
        compiler_params=_params(
            ("parallel",),
            4 * tm * w * 2 + tm * d * 2 + 2 * tm * d * 4,
            (N_BRANCH * d * d + N_BRANCH * w * d + d * d) * 2 + 6 * tm * d * 4),
        name="merge",
    )(ya, yb, yc, yd, xn, h, wgate, wb, wo)


def _ffn_kernel(h_ref, g_ref, wg_ref, wu_ref, wd_ref, gnext_ref, *out_refs, hidden_chunk, last):
    h = h_ref[...]
    hn = _rmsnorm(h, g_ref[...]).astype(BF16)
    hidden = wg_ref.shape[1]
    acc = h
    for c0 in range(0, hidden, hidden_chunk):
        cs = slice(c0, min(c0 + hidden_chunk, hidden))
        gate = _dot(hn, wg_ref[:, cs])
        up = _dot(hn, wu_ref[:, cs])
        act = (gate * _sigmoid(gate) * up).astype(BF16)
        acc = acc + _dot(act, wd_ref[cs, :])
    normed = _rmsnorm(acc, gnext_ref[...])
    if last:
        out_refs[0][...] = normed
    else:
        out_refs[0][...] = acc
        out_refs[1][...] = normed.astype(BF16)


def _ffn(h, g, wg, wu, wd, gnext, layer, next_layer, tm):
    t, d = h.shape
    hidden = wg.shape[-1]
    last = next_layer is None
    row = pl.BlockSpec((tm, d), lambda i: (i, 0))
    resident = functools.partial(pl.BlockSpec, pipeline_mode=pl.Buffered(1))
    if last:
        out_specs, out_shape = row, jax.ShapeDtypeStruct((t, d), F32)
    else:
        out_specs = (row, row)
        out_shape = (jax.ShapeDtypeStruct((t, d), F32), jax.ShapeDtypeStruct((t, d), BF16))
    gnext_row = 0 if last else next_layer
    return pl.pallas_call(
        functools.partial(_ffn_kernel, hidden_chunk=512, last=last),
        grid=(t // tm,),
        in_specs=[
            row,
            pl.BlockSpec((None, 1, d), lambda i: (layer, 0, 0)),
            resident((None, d, hidden), lambda i: (layer, 0, 0)),
            resident((None, d, hidden), lambda i: (layer, 0, 0)),
            resident((None, hidden, d), lambda i: (layer, 0, 0)),
            pl.BlockSpec((None, 1, d), lambda i: (gnext_row, 0, 0)),
        ],
        out_specs=out_specs,
        out_shape=out_shape,
        compiler_params=_params(("parallel",), 3 * tm * d * 4,
                                3 * d * hidden * 2 + 3 * tm * d * 4 + 3 * tm * 512 * 4),
        name="ffn",
    )(h, g, wg, wu, wd, gnext)


def kernel(x, norm_mix, w_in, a_w_gk, a_b_gk, a_norm, b_rel_bias, c_conv_w, c_conv_b, c_w_a, c_b_a,
           c_w_x, c_b_x, c_lambda, w_branch, w_out, norm_ffn, w_ffn_gate, w_ffn_up, w_ffn_down,
           norm_final):
    bsz, s, d = x.shape
    depth = w_in.shape[0]
    t = bsz * s
    w = BRANCH_WIDTH

    sizes = (A_HEADS * HEAD_DIM, A_HEADS * HEAD_DIM, A_HEADS * A_DV, A_RANK, A_HEADS * A_DV,
             w, w, w, w, w, w, w, w, N_BRANCH * d)
    offs = [0]
    for sz in sizes:
        offs.append(offs[-1] + sz)
    (o_aq, o_ak, o_av, o_ar, o_ag, o_bq, _, _, o_cg, _, o_dq, _, _, o_gate, o_end) = offs

    w_in_b = w_in.astype(BF16)
    w_qkv = jnp.concatenate(
        [w_in_b[:, :, o_av:o_ar], w_in_b[:, :, o_bq:o_cg], w_in_b[:, :, o_dq:o_gate]], axis=-1)
    w_f = jnp.concatenate(
        [w_in_b[:, :, o_ag:o_bq], w_in_b[:, :, o_cg:o_dq], w_in_b[:, :, o_aq:o_av],
         w_in_b[:, :, o_ar:o_ag], jnp.zeros((depth, d, 2 * LANES - A_RANK), BF16)], axis=-1)
    w_gate = w_in_b[:, :, o_gate:o_end]
    wgk = jnp.pad(a_w_gk, ((0, 0), (0, LANES - A_RANK), (0, 0))).astype(BF16)
    vec = lambda p: p.reshape(depth, 1, -1)
    bias = _band_bias(b_rel_bias)
    wa = _block_diag(c_w_a).astype(BF16)
    wx = _block_diag(c_w_x).astype(BF16)
    wb = w_branch.astype(BF16)
    wo = w_out.astype(BF16)
    wg = w_ffn_gate.astype(BF16)
    wu = w_ffn_up.astype(BF16)
    wd = w_ffn_down.astype(BF16)
    gf = norm_final.reshape(1, 1, d)

    h = x.reshape(t, d)
    tm_proj = min(1024, t)
    xn = _norm(h, vec(norm_mix), 0, tm_proj)
    for layer in range(depth):
        pj = _proj(xn, w_qkv, layer, BF16, tm_proj, QKV_COLS // 2).reshape(bsz, s, QKV_COLS)
        pf = _proj(xn, w_f, layer, F32, tm_proj, F_COLS // 2).reshape(bsz, s, F_COLS)
        ya = _gla(pf, pj, wgk, vec(a_b_gk), vec(a_norm), layer, 512)
        yb = _chunk_attn(pj, bias, layer)
        yc = _rglru(pf, c_conv_w, vec(c_conv_b), wa, vec(c_b_a), wx, vec(c_b_x), vec(c_lambda),
                    layer, 512)
        yd = _stick(pj)
        h = _merge(ya.reshape(t, w), yb.reshape(t, w), yc.reshape(t, w), yd.reshape(t, w),
                   xn, h, w_gate, wb, wo, layer, 512)
        if layer + 1 < depth:
            h, xn = _ffn(h, vec(norm_ffn), wg, wu, wd, vec(norm_mix), layer, layer + 1, 512)
        else:
            h = _ffn(h, vec(norm_ffn), wg, wu, wd, gf, layer, None, 512)
    return h.reshape(bsz, s, d)
```

```python
import functools

import jax
import jax.numpy as jnp
from jax import lax
from jax.experimental import pallas as pl
from jax.experimental.pallas import tpu as pltpu

F32 = jnp.float32
BF16 = jnp.bfloat16

EPS = 1e-6
CHUNK = 64
BRANCH_WIDTH = 512
N_BRANCH = 4
HEAD_DIM = 64
A_HEADS = 4
A_DV = 128
A_RANK = 16
A_GATE_NORM = 16.0
B_HEADS = 8
B_PREV_CHUNKS = 8
B_MAX_REL = 128
C_BLOCKS = 8
C_CONV = 4
C_POW = 8.0
D_HEADS = 8
MASK_VALUE = -1e30
EXP_UNDERFLOW = 104.0
ALWAYS_VISITED = 3

LANES = 128
SUBLANES = 8
VMEM_BYTES_V7X = 64 * 1024 * 1024

QKV_COLS = 7 * BRANCH_WIDTH
QKV_A_V, QKV_B_Q, QKV_B_K, QKV_B_V, QKV_D_Q, QKV_D_K, QKV_D_V = range(7)
F_A_G, F_C_G, F_C_X, F_A_QK = range(4)
F_A_R_COL = 4 * BRANCH_WIDTH
F_COLS = F_A_R_COL + 2 * LANES

Q_BLOCK = 128
B_Q_BLOCKS_PER_STEP = 4
D_Q_BLOCKS_PER_STEP = 2
BAND = (B_PREV_CHUNKS + 2) * CHUNK
B_PAD = B_PREV_CHUNKS * CHUNK
RAMP_WIDTH = BAND + Q_BLOCK


def _dot(a, b):
    return jnp.dot(a, b, preferred_element_type=F32)


def _dot_nt(a, b):
    return lax.dot_general(a, b, (((1,), (1,)), ((), ())), preferred_element_type=F32)


def _dot_tn(a, b):
    return lax.dot_general(a, b, (((0,), (0,)), ((), ())), preferred_element_type=F32)


def _two_bf16_terms(x):
    hi = x.astype(BF16)
    lo = (x - hi.astype(F32)).astype(BF16)
    return hi, lo


def _log1p_exp_neg_abs(z):
    return jnp.log1p(jnp.exp(-jnp.abs(z)))


def _log_sigmoid(z):
    return jnp.minimum(z, 0.0) - _log1p_exp_neg_abs(z)


def _sigmoid(z):
    return 1.0 / (1.0 + jnp.exp(-z))


VMEM_REQUEST_BYTES = VMEM_BYTES_V7X - 8 * 1024 * 1024


def _params(*semantics):
    return pltpu.CompilerParams(dimension_semantics=semantics, vmem_limit_bytes=VMEM_REQUEST_BYTES)


def _rmsnorm(x, g):
    return x * lax.rsqrt(jnp.mean(x * x, axis=-1, keepdims=True) + EPS) * g


def _norm_kernel(x_ref, g_ref, o_ref):
    o_ref[...] = _rmsnorm(x_ref[...], g_ref[...]).astype(o_ref.dtype)


def _norm(x, g, layer, tm):
    t, d = x.shape
    return pl.pallas_call(
        _norm_kernel,
        grid=(t // tm,),
        in_specs=[
            pl.BlockSpec((tm, d), lambda i: (i, 0)),
            pl.BlockSpec((None, 1, d), lambda i: (layer, 0, 0)),
        ],
        out_specs=pl.BlockSpec((tm, d), lambda i: (i, 0)),
        out_shape=jax.ShapeDtypeStruct((t, d), BF16),
        compiler_params=_params("parallel"),
        name="norm",
    )(x, g)


def _proj_kernel(x_ref, w_ref, o_ref):
    o_ref[...] = _dot(x_ref[...], w_ref[...]).astype(o_ref.dtype)


def _proj(xn, w, layer, out_dtype, tm, tn):
    t, d = xn.shape
    n = w.shape[-1]
    return pl.pallas_call(
        _proj_kernel,
        grid=(t // tm, n // tn),
        in_specs=[
            pl.BlockSpec((tm, d), lambda i, j: (i, 0)),
            pl.BlockSpec((None, d, tn), lambda i, j: (layer, 0, j)),
        ],
        out_specs=pl.BlockSpec((tm, tn), lambda i, j: (i, j)),
        out_shape=jax.ShapeDtypeStruct((t, n), out_dtype),
        compiler_params=_params("parallel", "arbitrary"),
        name="proj",
    )(xn, w)


def _gla_kernel(qk_ref, v_ref, g_ref, r_ref, wgk_ref, bgk_ref, ng_ref, o_ref, state_ref, *,
                n_chunks):
    dk_all = A_HEADS * HEAD_DIM
    chunks = range(n_chunks)
    rows = [slice(c * CHUNK, (c + 1) * CHUNK) for c in chunks]

    @pl.when(pl.program_id(1) == 0)
    def _():
        state_ref[...] = jnp.zeros_like(state_ref)

    tri2 = jnp.where(lax.broadcasted_iota(jnp.int32, (CHUNK, 2 * CHUNK), 0)
                     >= lax.broadcasted_iota(jnp.int32, (CHUNK, 2 * CHUNK), 1) % CHUNK,
                     1.0, 0.0).astype(BF16)
    own_head = (lax.broadcasted_iota(jnp.int32, (dk_all, dk_all), 0) // HEAD_DIM
                == lax.broadcasted_iota(jnp.int32, (dk_all, dk_all), 1) // HEAD_DIM)

    def head_stacked(x):
        return jnp.where(own_head, jnp.concatenate([x] * A_HEADS, axis=0), 0.0).astype(BF16)

    r = r_ref[...].astype(BF16)
    gk = _log_sigmoid(_dot(r, wgk_ref[...]) + bgk_ref[...]) * (1.0 / A_GATE_NORM)
    gk_hi, gk_lo = _two_bf16_terms(gk)
    k = qk_ref[:, dk_all:]
    q = qk_ref[:, :dk_all] * (HEAD_DIM ** -0.5)

    k_dec, q_st, v_st, decay = [], [], [], []
    for c in chunks:
        cum = _dot(tri2, jnp.concatenate([gk_hi[rows[c]], gk_lo[rows[c]]], axis=0))
        tot = cum[CHUNK - 1:CHUNK, :]
        k_dec.append(head_stacked(k[rows[c]] * jnp.exp(tot - cum)))
        q_st.append(head_stacked(q[rows[c]]))
        v_c = v_ref[rows[c], :]
        v_st.append(jnp.concatenate([v_c[:, h * A_DV:(h + 1) * A_DV] for h in range(A_HEADS)], axis=0))
        decay.append(jnp.exp(tot))
    kv = [_dot_tn(v_st[c], k_dec[c]) for c in chunks]

    state = state_ref[...]
    outs = []
    for c in chunks:
        state = state * decay[c] + kv[c]
        outs.append(_dot_nt(q_st[c], state.astype(BF16)))
    state_ref[...] = state

    for h in range(A_HEADS):
        cols = slice(h * A_DV, (h + 1) * A_DV)
        o = jnp.concatenate([outs[c][h * CHUNK:(h + 1) * CHUNK] for c in chunks], axis=0)
        o = o * lax.rsqrt(jnp.mean(o * o, axis=-1, keepdims=True) + EPS) * ng_ref[...]
        g = g_ref[:, cols]
        o_ref[:, cols] = (o * (g * _sigmoid(g))).astype(o_ref.dtype)


def _gla(pf, pj, wgk, bgk, ng, layer, tq):
    b, s, _ = pf.shape
    w = BRANCH_WIDTH
    return pl.pallas_call(
        functools.partial(_gla_kernel, n_chunks=tq // CHUNK),
        grid=(b, s // tq),
        in_specs=[
            pl.BlockSpec((None, tq, w), lambda bi, i: (bi, i, F_A_QK)),
            pl.BlockSpec((None, tq, w), lambda bi, i: (bi, i, QKV_A_V)),
            pl.BlockSpec((None, tq, w), lambda bi, i: (bi, i, F_A_G)),
            pl.BlockSpec((None, tq, LANES), lambda bi, i: (bi, i, F_A_R_COL // LANES)),
            pl.BlockSpec((None, LANES, A_HEADS * HEAD_DIM), lambda bi, i: (layer, 0, 0)),
            pl.BlockSpec((None, 1, A_HEADS * HEAD_DIM), lambda bi, i: (layer, 0, 0)),
            pl.BlockSpec((None, 1, A_DV), lambda bi, i: (layer, 0, 0)),
        ],
        out_specs=pl.BlockSpec((None, tq, w), lambda bi, i: (bi, i, 0)),
        out_shape=jax.ShapeDtypeStruct((b, s, w), BF16),
        scratch_shapes=[pltpu.VMEM((A_DV, A_HEADS * HEAD_DIM), F32)],
        compiler_params=_params("parallel", "arbitrary"),
        name="mixer_a_gla",
    )(pf, pj, pf, pf, wgk, bgk, ng)


def _chunk_attn_kernel(q_ref, k_ref, v_ref, ramp_ref, o_ref, qs_ref, bias_ref, kpad_ref, vfirst_ref,
                       vsecond_ref):
    step = pl.program_id(1)
    s = k_ref.shape[0]
    n_pairs = B_HEADS // 2
    init_rows = 2 * Q_BLOCK

    @pl.when(step == 0)
    def _():
        zeros = jnp.zeros((B_PAD, BRANCH_WIDTH), BF16)
        kpad_ref[0:B_PAD, :] = zeros
        vfirst_ref[0:B_PAD, :] = zeros
        vsecond_ref[0:B_PAD, :] = zeros
        kpad_ref[B_PAD:B_PAD + s, :] = k_ref[...]
        lane = lax.broadcasted_iota(jnp.int32, (init_rows, BRANCH_WIDTH), 1)
        first = (lane % LANES) < HEAD_DIM
        for r0 in range(0, s, init_rows):
            v = v_ref[r0:r0 + init_rows, :].astype(F32)
            vfirst_ref[B_PAD + r0:B_PAD + r0 + init_rows, :] = jnp.where(first, v, 0.0).astype(BF16)
            vsecond_ref[B_PAD + r0:B_PAD + r0 + init_rows, :] = jnp.where(first, 0.0, v).astype(BF16)
        qi = lax.broadcasted_iota(jnp.int32, (Q_BLOCK, BAND), 0) // CHUNK
        kj = lax.broadcasted_iota(jnp.int32, (Q_BLOCK, BAND), 1) // CHUNK
        in_band = (kj >= qi) & (kj <= qi + B_PREV_CHUNKS)
        for h in range(B_HEADS):
            ramp = jnp.broadcast_to(ramp_ref[h:h + 1, :], (Q_BLOCK, RAMP_WIDTH))
            skew = pltpu.roll(ramp, RAMP_WIDTH - (Q_BLOCK - 1), axis=1, stride=1, stride_axis=0)
            half = slice((h % 2) * Q_BLOCK, (h % 2 + 1) * Q_BLOCK)
            bias_ref[h // 2, half, :] = jnp.where(in_band, skew[:, :BAND], MASK_VALUE)

    first_head = lax.broadcasted_iota(jnp.int32, (Q_BLOCK, LANES), 1) < HEAD_DIM

    def attend(sub, has_padding):
        i = step * B_Q_BLOCKS_PER_STEP + sub
        rows = slice(sub * Q_BLOCK, (sub + 1) * Q_BLOCK)
        q = q_ref[rows, :].astype(F32) * (HEAD_DIM ** -0.5)
        for p in range(n_pairs):
            q_pair = q[:, p * LANES:(p + 1) * LANES]
            qs_ref[p, 0:Q_BLOCK, :] = jnp.where(first_head, q_pair, 0.0).astype(BF16)
            qs_ref[p, Q_BLOCK:2 * Q_BLOCK, :] = jnp.where(first_head, 0.0, q_pair).astype(BF16)
        window = pl.ds(pl.multiple_of(i * Q_BLOCK, Q_BLOCK), BAND)
        pairs = range(n_pairs)
        cols = [slice(p * LANES, (p + 1) * LANES) for p in pairs]
        sc = [_dot_nt(qs_ref[p], kpad_ref[window, cols[p]]) + bias_ref[p] for p in pairs]
        if has_padding:
            in_seq = lax.broadcasted_iota(jnp.int32, (2 * Q_BLOCK, BAND), 1) >= B_PAD - i * Q_BLOCK
            sc = [jnp.where(in_seq, x, MASK_VALUE) for x in sc]
        e = [jnp.exp(x - jnp.max(x, axis=-1, keepdims=True)) for x in sc]
        prob = [(x * (1.0 / jnp.sum(x, axis=-1, keepdims=True))).astype(BF16) for x in e]
        for p in pairs:
            both = jnp.concatenate([prob[p][:Q_BLOCK], prob[p][Q_BLOCK:]], axis=1)
            values = jnp.concatenate([vfirst_ref[window, cols[p]], vsecond_ref[window, cols[p]]], axis=0)
            o_ref[rows, cols[p]] = _dot(both, values).astype(o_ref.dtype)

    padded_steps = B_PAD // (Q_BLOCK * B_Q_BLOCKS_PER_STEP)

    @pl.when(step < padded_steps)
    def _():
        for sub in range(B_Q_BLOCKS_PER_STEP):
            attend(sub, True)

    @pl.when(step >= padded_steps)
    def _():
        for sub in range(B_Q_BLOCKS_PER_STEP):
            attend(sub, False)


def _chunk_attn(pj, ramp, layer):
    b, s, _ = pj.shape
    w = BRANCH_WIDTH
    q_rows = Q_BLOCK * B_Q_BLOCKS_PER_STEP
    return pl.pallas_call(
        _chunk_attn_kernel,
        grid=(b, s // q_rows),
        in_specs=[
            pl.BlockSpec((None, q_rows, w), lambda bi, i: (bi, i, QKV_B_Q)),
            pl.BlockSpec((None, s, w), lambda bi, i: (bi, 0, QKV_B_K)),
            pl.BlockSpec((None, s, w), lambda bi, i: (bi, 0, QKV_B_V)),
            pl.BlockSpec((None, B_HEADS, RAMP_WIDTH), lambda bi, i: (layer, 0, 0)),
        ],
        out_specs=pl.BlockSpec((None, q_rows, w), lambda bi, i: (bi, i, 0)),
        out_shape=jax.ShapeDtypeStruct((b, s, w), BF16),
        scratch_shapes=[
            pltpu.VMEM((B_HEADS // 2, 2 * Q_BLOCK, LANES), BF16),
            pltpu.VMEM((B_HEADS // 2, 2 * Q_BLOCK, BAND), F32),
            pltpu.VMEM((s + B_PAD, w), BF16),
            pltpu.VMEM((s + B_PAD, w), BF16),
            pltpu.VMEM((s + B_PAD, w), BF16),
        ],
        compiler_params=_params("parallel", "arbitrary"),
        name="mixer_b_chunk_attn",
    )(pj, pj, pj, ramp)


def _bias_ramp(rel_table):
    idx = jnp.clip((BAND - 1) - jnp.arange(RAMP_WIDTH), -B_MAX_REL, B_MAX_REL) + B_MAX_REL
    return rel_table.astype(F32)[:, :, idx]


def _rglru_kernel(g_ref, x_ref, cw_ref, cb_ref, wa_ref, ba_ref, wx_ref, bx_ref, lam_ref, o_ref,
                  xe_ref, a_ref, b_ref, h_ref, carry_ref):
    ts = x_ref.shape[0]
    halo = SUBLANES

    @pl.when(pl.program_id(1) == 0)
    def _():
        xe_ref[0:halo, :] = jnp.zeros((halo, BRANCH_WIDTH), F32)
        carry_ref[...] = jnp.zeros_like(carry_ref)

    x = x_ref[...]
    xe_ref[halo:halo + ts, :] = x
    xc = cb_ref[...]
    for j in range(C_CONV - 1):
        lag = C_CONV - 1 - j
        xc = xc + xe_ref[halo - lag:halo - lag + ts, :] * cw_ref[j:j + 1, :]
    xc = xc + x * cw_ref[C_CONV - 1:C_CONV, :]
    xe_ref[0:halo, :] = xe_ref[ts:ts + halo, :]

    xcb = xc.astype(BF16)
    r = _sigmoid(_dot(xcb, wa_ref[...]) + ba_ref[...])
    gate_i = _sigmoid(_dot(xcb, wx_ref[...]) + bx_ref[...])
    lam = lam_ref[...]
    softplus_neg_lam = jnp.maximum(-lam, 0.0) + _log1p_exp_neg_abs(lam)
    log_a = -C_POW * r * softplus_neg_lam
    a = jnp.exp(log_a)
    a_ref[...] = a
    b_ref[...] = jnp.sqrt(-jnp.tanh(log_a) * (a * a + 1.0)) * (gate_i * xc)

    sub = lax.broadcasted_iota(jnp.int32, (SUBLANES, BRANCH_WIDTH), 0)

    def group(gi, carry):
        rows = pl.ds(pl.multiple_of(gi * SUBLANES, SUBLANES), SUBLANES)
        a = a_ref[rows, :]
        bb = b_ref[rows, :]
        for d in (1, 2, 4):
            a_prev = pltpu.roll(a, d, axis=0)
            b_prev = pltpu.roll(bb, d, axis=0)
            has_prev = sub >= d
            bb = jnp.where(has_prev, a * b_prev + bb, bb)
            a = jnp.where(has_prev, a * a_prev, a)
        h = a * carry + bb
        h_ref[rows, :] = h
        return jnp.broadcast_to(h[SUBLANES - 1:SUBLANES, :], (SUBLANES, BRANCH_WIDTH))

    carry_ref[...] = lax.fori_loop(0, ts // SUBLANES, group, carry_ref[...], unroll=8)
    o_ref[...] = (jax.nn.gelu(g_ref[...]) * h_ref[...]).astype(o_ref.dtype)


def _rglru(pf, cw, cb, wa, ba, wx, bx, lam, layer, ts):
    b, s, _ = pf.shape
    w = BRANCH_WIDTH
    vec = pl.BlockSpec((None, 1, w), lambda bi, i: (layer, 0, 0))
    mat = pl.BlockSpec((None, w, w), lambda bi, i: (layer, 0, 0))
    return pl.pallas_call(
        _rglru_kernel,
        grid=(b, s // ts),
        in_specs=[
            pl.BlockSpec((None, ts, w), lambda bi, i: (bi, i, F_C_G)),
            pl.BlockSpec((None, ts, w), lambda bi, i: (bi, i, F_C_X)),
            pl.BlockSpec((None, C_CONV, w), lambda bi, i: (layer, 0, 0)),
            vec, mat, vec, mat, vec, vec,
        ],
        out_specs=pl.BlockSpec((None, ts, w), lambda bi, i: (bi, i, 0)),
        out_shape=jax.ShapeDtypeStruct((b, s, w), BF16),
        scratch_shapes=[
            pltpu.VMEM((ts + 2 * SUBLANES, w), F32),
            pltpu.VMEM((ts, w), F32),
            pltpu.VMEM((ts, w), F32),
            pltpu.VMEM((ts, w), F32),
            pltpu.VMEM((SUBLANES, w), F32),
        ],
        compiler_params=_params("parallel", "arbitrary"),
        name="mixer_c_rglru",
    )(pf, pf, cw, cb, wa, ba, wx, bx, lam)


def _block_diag(wb):
    l, g, n, _ = wb.shape
    eye = jnp.eye(g, dtype=wb.dtype)
    return (wb[:, :, :, None, :] * eye[None, :, None, :, None]).reshape(l, g * n, g * n)


def _stick_kernel(q_ref, k_ref, v_ref, o_ref, qs_ref, vs_ref, later_ref, acc_ref):
    step = pl.program_id(1)
    n_pairs = D_HEADS // 2
    n_key_blocks = k_ref.shape[0] // Q_BLOCK
    first_head = lax.broadcasted_iota(jnp.int32, (Q_BLOCK, LANES), 1) < HEAD_DIM

    @pl.when(step == 0)
    def _():
        for jb in range(n_key_blocks):
            for p in range(n_pairs):
                v = v_ref[jb * Q_BLOCK:(jb + 1) * Q_BLOCK, p * LANES:(p + 1) * LANES].astype(F32)
                vs_ref[jb, p, 0:Q_BLOCK, :] = jnp.where(first_head, v, 0.0).astype(BF16)
                vs_ref[jb, p, Q_BLOCK:2 * Q_BLOCK, :] = jnp.where(first_head, 0.0, v).astype(BF16)

    row = lax.broadcasted_iota(jnp.int32, (2 * Q_BLOCK, Q_BLOCK), 0) % Q_BLOCK
    col = lax.broadcasted_iota(jnp.int32, (2 * Q_BLOCK, Q_BLOCK), 1)
    strictly_before = col < row
    rr = lax.broadcasted_iota(jnp.int32, (2 * Q_BLOCK, 2 * Q_BLOCK), 0) % Q_BLOCK
    cc = lax.broadcasted_iota(jnp.int32, (2 * Q_BLOCK, 2 * Q_BLOCK), 1)
    suffix_and_total = jnp.where((rr > cc) | (cc >= Q_BLOCK), 1.0, 0.0).astype(BF16)

    def key_blocks(js, from_diagonal):
        pairs = range(n_pairs)
        blocks = range(len(js))
        ks = [pl.multiple_of(j * Q_BLOCK, Q_BLOCK) for j in js]
        both = [(b, p) for b in blocks for p in pairs]
        z, neg_log1m, log_sig, sums, att = {}, {}, {}, {}, {}
        for b, p in both:
            z[b, p] = _dot_nt(qs_ref[p], k_ref[pl.ds(ks[b], Q_BLOCK), p * LANES:(p + 1) * LANES])
        for b, p in both:
            t = jnp.maximum(z[b, p], 0.0) + jnp.log(1.0 + jnp.exp(-jnp.abs(z[b, p])))
            log_sig[b, p] = z[b, p] - t
            masked = from_diagonal and b == 0
            neg_log1m[b, p] = jnp.where(strictly_before, t, 0.0) if masked else t
        for b, p in both:
            hi, lo = _two_bf16_terms(neg_log1m[b, p])
            sums[b, p] = _dot(jnp.concatenate([hi, lo], axis=1), suffix_and_total)
        tails, totals = {}, []
        for p in pairs:
            later = None if from_diagonal else later_ref[p]
            for b in blocks:
                tail = sums[b, p][:, :Q_BLOCK]
                total = sums[b, p][:, Q_BLOCK:]
                if later is not None:
                    tail = tail + later
                    total = total + later
                tails[b, p] = tail
                later = total
            later_ref[p] = later
            totals.append(later)
        least = jnp.min(jnp.minimum(jnp.minimum(totals[0], totals[1]), jnp.minimum(totals[2], totals[3])))
        for b, p in both:
            a = jnp.exp(log_sig[b, p] - tails[b, p])
            if from_diagonal and b == 0:
                a = jnp.where(strictly_before, a, 0.0)
            a = a.astype(BF16)
            att[b, p] = jnp.concatenate([a[:Q_BLOCK], a[Q_BLOCK:]], axis=1)
        for p in pairs:
            pv = None if from_diagonal else acc_ref[p]
            for b in blocks:
                term = _dot(att[b, p], vs_ref[js[b], p])
                pv = term if pv is None else pv + term
            acc_ref[p] = pv
        return least

    def query_block(sub):
        i = step * D_Q_BLOCKS_PER_STEP + sub
        rows = slice(sub * Q_BLOCK, (sub + 1) * Q_BLOCK)
        q = q_ref[rows, :].astype(F32) * (HEAD_DIM ** -0.5)
        for p in range(n_pairs):
            q_pair = q[:, p * LANES:(p + 1) * LANES]
            qs_ref[p, 0:Q_BLOCK, :] = jnp.where(first_head, q_pair, 0.0).astype(BF16)
            qs_ref[p, Q_BLOCK:2 * Q_BLOCK, :] = jnp.where(first_head, 0.0, q_pair).astype(BF16)

        least = lax.cond(
            i > 1, lambda: key_blocks([i - b for b in range(ALWAYS_VISITED)], True),
            lambda: lax.cond(i > 0, lambda: key_blocks([i, i - 1], True), lambda: key_blocks([i], True)))

        def more(carry):
            jj, least = carry
            return jnp.logical_and(jj < i, least < EXP_UNDERFLOW)

        def earlier(carry):
            jj, _ = carry
            return jj + 1, key_blocks([i - 1 - jj], False)

        lax.while_loop(more, earlier, (jnp.int32(ALWAYS_VISITED - 1), least))

        for p in range(n_pairs):
            o_ref[rows, p * LANES:(p + 1) * LANES] = acc_ref[p].astype(o_ref.dtype)

    for sub in range(D_Q_BLOCKS_PER_STEP):
        query_block(sub)


def _stick(pj):
    b, s, _ = pj.shape
    w = BRANCH_WIDTH
    q_rows = Q_BLOCK * D_Q_BLOCKS_PER_STEP
    return pl.pallas_call(
        _stick_kernel,
        grid=(b, s // q_rows),
        in_specs=[
            pl.BlockSpec((None, q_rows, w), lambda bi, i: (bi, i, QKV_D_Q)),
            pl.BlockSpec((None, s, w), lambda bi, i: (bi, 0, QKV_D_K)),
            pl.BlockSpec((None, s, w), lambda bi, i: (bi, 0, QKV_D_V)),
        ],
        out_specs=pl.BlockSpec((None, q_rows, w), lambda bi, i: (bi, i, 0)),
        out_shape=jax.ShapeDtypeStruct((b, s, w), BF16),
        scratch_shapes=[
            pltpu.VMEM((D_HEADS // 2, 2 * Q_BLOCK, LANES), BF16),
            pltpu.VMEM((s // Q_BLOCK, D_HEADS // 2, 2 * Q_BLOCK, LANES), BF16),
            pltpu.VMEM((D_HEADS // 2, 2 * Q_BLOCK, LANES), F32),
            pltpu.VMEM((D_HEADS // 2, Q_BLOCK, LANES), F32),
        ],
        compiler_params=_params("parallel", "arbitrary"),
        name="mixer_d_stick",
    )(pj, pj, pj)


def _merge_kernel(ya_ref, yb_ref, yc_ref, yd_ref, xn_ref, h_ref, wgate_ref, wb_ref, wo_ref, o_ref):
    d = h_ref.shape[1]
    xn = xn_ref[...]
    mixed = None
    for n, y_ref in enumerate((ya_ref, yb_ref, yc_ref, yd_ref)):
        gate = _sigmoid(_dot(xn, wgate_ref[:, n * d:(n + 1) * d]))
        term = gate * _dot(y_ref[...], wb_ref[n])
        mixed = term if mixed is None else mixed + term
    o_ref[...] = h_ref[...] + _dot(mixed.astype(BF16), wo_ref[...])


def _merge(ya, yb, yc, yd, xn, h, wgate, wb, wo, layer, tm):
    t, d = h.shape
    w = BRANCH_WIDTH
    y_spec = pl.BlockSpec((tm, w), lambda i: (i, 0))
    row = pl.BlockSpec((tm, d), lambda i: (i, 0))
    resident = functools.partial(pl.BlockSpec, pipeline_mode=pl.Buffered(1))
    return pl.pallas_call(
        _merge_kernel,
        grid=(t // tm,),
        in_specs=[
            y_spec, y_spec, y_spec, y_spec, row, row,
            resident((None, d, N_BRANCH * d), lambda i: (layer, 0, 0)),
            resident((None, N_BRANCH, w, d), lambda i: (layer, 0, 0, 0)),
            resident((None, d, d), lambda i: (layer, 0, 0)),
        ],
        out_specs=row,
        out_shape=jax.ShapeDtypeStruct((t, d), F32),
        compiler_params=_params("parallel"),
        name="merge",
    )(ya, yb, yc, yd, xn, h, wgate, wb, wo)


def _ffn_kernel(h_ref, g_ref, wg_ref, wu_ref, wd_ref, gnext_ref, *out_refs, hidden_chunk, last):
    h = h_ref[...]
    hn = _rmsnorm(h, g_ref[...]).astype(BF16)
    hidden = wg_ref.shape[1]
    acc = h
    for c0 in range(0, hidden, hidden_chunk):
        cs = slice(c0, min(c0 + hidden_chunk, hidden))
        gate = _dot(hn, wg_ref[:, cs])
        up = _dot(hn, wu_ref[:, cs])
        act = (gate * _sigmoid(gate) * up).astype(BF16)
        acc = acc + _dot(act, wd_ref[cs, :])
    normed = _rmsnorm(acc, gnext_ref[...])
    if last:
        out_refs[0][...] = normed
    else:
        out_refs[0][...] = acc
        out_refs[1][...] = normed.astype(BF16)


def _ffn(h, g, wg, wu, wd, gnext, layer, next_layer, tm):
    t, d = h.shape
    hidden = wg.shape[-1]
    last = next_layer is None
    row = pl.BlockSpec((tm, d), lambda i: (i, 0))
    resident = functools.partial(pl.BlockSpec, pipeline_mode=pl.Buffered(1))
    if last:
        out_specs, out_shape = row, jax.ShapeDtypeStruct((t, d), F32)
    else:
        out_specs = (row, row)
        out_shape = (jax.ShapeDtypeStruct((t, d), F32), jax.ShapeDtypeStruct((t, d), BF16))
    gnext_row = 0 if last else next_layer
    return pl.pallas_call(
        functools.partial(_ffn_kernel, hidden_chunk=512, last=last),
        grid=(t // tm,),
        in_specs=[
            row,
            pl.BlockSpec((None, 1, d), lambda i: (layer, 0, 0)),
            resident((None, d, hidden), lambda i: (layer, 0, 0)),
            resident((None, d, hidden), lambda i: (layer, 0, 0)),
            resident((None, hidden, d), lambda i: (layer, 0, 0)),
            pl.BlockSpec((None, 1, d), lambda i: (gnext_row, 0, 0)),
        ],
        out_specs=out_specs,
        out_shape=out_shape,
        compiler_params=_params("parallel"),
        name="ffn",
    )(h, g, wg, wu, wd, gnext)


def kernel(x, norm_mix, w_in, a_w_gk, a_b_gk, a_norm, b_rel_bias, c_conv_w, c_conv_b, c_w_a, c_b_a,
           c_w_x, c_b_x, c_lambda, w_branch, w_out, norm_ffn, w_ffn_gate, w_ffn_up, w_ffn_down,
           norm_final):
    bsz, s, d = x.shape
    depth = w_in.shape[0]
    t = bsz * s
    w = BRANCH_WIDTH

    sizes = (A_HEADS * HEAD_DIM, A_HEADS * HEAD_DIM, A_HEADS * A_DV, A_RANK, A_HEADS * A_DV,
             w, w, w, w, w, w, w, w, N_BRANCH * d)
    offs = [0]
    for sz in sizes:
        offs.append(offs[-1] + sz)
    (o_aq, o_ak, o_av, o_ar, o_ag, o_bq, _, _, o_cg, _, o_dq, _, _, o_gate, o_end) = offs

    w_in_b = w_in.astype(BF16)
    w_qkv = jnp.concatenate(
        [w_in_b[:, :, o_av:o_ar], w_in_b[:, :, o_bq:o_cg], w_in_b[:, :, o_dq:o_gate]], axis=-1)
    w_f = jnp.concatenate(
        [w_in_b[:, :, o_ag:o_bq], w_in_b[:, :, o_cg:o_dq], w_in_b[:, :, o_aq:o_av],
         w_in_b[:, :, o_ar:o_ag], jnp.zeros((depth, d, 2 * LANES - A_RANK), BF16)], axis=-1)
    w_gate = w_in_b[:, :, o_gate:o_end]
    wgk = jnp.pad(a_w_gk, ((0, 0), (0, LANES - A_RANK), (0, 0))).astype(BF16)
    vec = lambda p: p.reshape(depth, 1, -1)
    ramp = _bias_ramp(b_rel_bias)
    wa = _block_diag(c_w_a).astype(BF16)
    wx = _block_diag(c_w_x).astype(BF16)
    wb = w_branch.astype(BF16)
    wo = w_out.astype(BF16)
    wg = w_ffn_gate.astype(BF16)
    wu = w_ffn_up.astype(BF16)
    wd = w_ffn_down.astype(BF16)
    gf = norm_final.reshape(1, 1, d)

    h = x.reshape(t, d)
    tm_proj = min(1024, t)
    xn = _norm(h, vec(norm_mix), 0, tm_proj)
    for layer in range(depth):
        pj = _proj(xn, w_qkv, layer, BF16, tm_proj, QKV_COLS // 2).reshape(bsz, s, QKV_COLS)
        pf = _proj(xn, w_f, layer, F32, tm_proj, F_COLS // 2).reshape(bsz, s, F_COLS)
        ya = _gla(pf, pj, wgk, vec(a_b_gk), vec(a_norm), layer, 512)
        yb = _chunk_attn(pj, ramp, layer)
        yc = _rglru(pf, c_conv_w, vec(c_conv_b), wa, vec(c_b_a), wx, vec(c_b_x), vec(c_lambda),
                    layer, 512)
        yd = _stick(pj)
        h = _merge(ya.reshape(t, w), yb.reshape(t, w), yc.reshape(t, w), yd.reshape(t, w),
                   xn, h, w_gate, wb, wo, layer, 512)
        if layer + 1 < depth:
            h, xn = _ffn(h, vec(norm_ffn), wg, wu, wd, vec(norm_mix), layer, layer + 1, 512)
        else:
            h = _ffn(h, vec(norm_ffn), wg, wu, wd, gf, layer, None, 512)
    return h.reshape(bsz, s, d)
```

```python
import functools

import jax
import jax.numpy as jnp
from jax import lax
from jax.experimental import pallas as pl
from jax.experimental.pallas import tpu as pltpu

F32 = jnp.float32
BF16 = jnp.bfloat16

EPS = 1e-6
CHUNK = 64
BRANCH_WIDTH = 512
N_BRANCH = 4
HEAD_DIM = 64
A_HEADS = 4
A_DV = 128
A_RANK = 16
A_GATE_NORM = 16.0
B_HEADS = 8
B_PREV_CHUNKS = 8
B_MAX_REL = 128
C_BLOCKS = 8
C_CONV = 4
C_POW = 8.0
D_HEADS = 8
MASK_VALUE = -1e30
EXP_UNDERFLOW = 104.0
ALWAYS_VISITED = 3

LANES = 128
SUBLANES = 8
VMEM_BYTES_V7X = 64 * 1024 * 1024

QKV_COLS = 7 * BRANCH_WIDTH
QKV_A_V, QKV_B_Q, QKV_B_K, QKV_B_V, QKV_D_Q, QKV_D_K, QKV_D_V = range(7)
F_A_G, F_C_G, F_C_X, F_A_QK = range(4)
F_A_R_COL = 4 * BRANCH_WIDTH
F_COLS = F_A_R_COL + 2 * LANES

Q_BLOCK = 128
B_Q_BLOCKS_PER_STEP = 4
D_Q_BLOCKS_PER_STEP = 2
BAND = (B_PREV_CHUNKS + 2) * CHUNK
B_PAD = B_PREV_CHUNKS * CHUNK
RAMP_WIDTH = BAND + Q_BLOCK


def _dot(a, b):
    return jnp.dot(a, b, preferred_element_type=F32)


def _dot_nt(a, b):
    return lax.dot_general(a, b, (((1,), (1,)), ((), ())), preferred_element_type=F32)


def _dot_tn(a, b):
    return lax.dot_general(a, b, (((0,), (0,)), ((), ())), preferred_element_type=F32)


def _two_bf16_terms(x):
    hi = x.astype(BF16)
    lo = (x - hi.astype(F32)).astype(BF16)
    return hi, lo


def _log1p_exp_neg_abs(z):
    return jnp.log1p(jnp.exp(-jnp.abs(z)))


def _log_sigmoid(z):
    return jnp.minimum(z, 0.0) - _log1p_exp_neg_abs(z)


def _sigmoid(z):
    return 1.0 / (1.0 + jnp.exp(-z))


VMEM_REQUEST_BYTES = VMEM_BYTES_V7X - 8 * 1024 * 1024


def _params(*semantics):
    return pltpu.CompilerParams(dimension_semantics=semantics, vmem_limit_bytes=VMEM_REQUEST_BYTES)


def _rmsnorm(x, g):
    return x * lax.rsqrt(jnp.mean(x * x, axis=-1, keepdims=True) + EPS) * g


def _norm_kernel(x_ref, g_ref, o_ref):
    o_ref[...] = _rmsnorm(x_ref[...], g_ref[...]).astype(o_ref.dtype)


def _norm(x, g, layer, tm):
    t, d = x.shape
    return pl.pallas_call(
        _norm_kernel,
        grid=(t // tm,),
        in_specs=[
            pl.BlockSpec((tm, d), lambda i: (i, 0)),
            pl.BlockSpec((None, 1, d), lambda i: (layer, 0, 0)),
        ],
        out_specs=pl.BlockSpec((tm, d), lambda i: (i, 0)),
        out_shape=jax.ShapeDtypeStruct((t, d), BF16),
        compiler_params=_params("parallel"),
        name="norm",
    )(x, g)


def _proj_kernel(x_ref, w_ref, o_ref):
    o_ref[...] = _dot(x_ref[...], w_ref[...]).astype(o_ref.dtype)


def _proj(xn, w, layer, out_dtype, tm, tn):
    t, d = xn.shape
    n = w.shape[-1]
    return pl.pallas_call(
        _proj_kernel,
        grid=(t // tm, n // tn),
        in_specs=[
            pl.BlockSpec((tm, d), lambda i, j: (i, 0)),
            pl.BlockSpec((None, d, tn), lambda i, j: (layer, 0, j)),
        ],
        out_specs=pl.BlockSpec((tm, tn), lambda i, j: (i, j)),
        out_shape=jax.ShapeDtypeStruct((t, n), out_dtype),
        compiler_params=_params("parallel", "arbitrary"),
        name="proj",
    )(xn, w)


def _gla_kernel(qk_ref, v_ref, g_ref, r_ref, wgk_ref, bgk_ref, ng_ref, o_ref, state_ref, *,
                n_chunks):
    dk_all = A_HEADS * HEAD_DIM
    chunks = range(n_chunks)
    rows = [slice(c * CHUNK, (c + 1) * CHUNK) for c in chunks]

    @pl.when(pl.program_id(1) == 0)
    def _():
        state_ref[...] = jnp.zeros_like(state_ref)

    tri2 = jnp.where(lax.broadcasted_iota(jnp.int32, (CHUNK, 2 * CHUNK), 0)
                     >= lax.broadcasted_iota(jnp.int32, (CHUNK, 2 * CHUNK), 1) % CHUNK,
                     1.0, 0.0).astype(BF16)
    own_head = (lax.broadcasted_iota(jnp.int32, (dk_all, dk_all), 0) // HEAD_DIM
                == lax.broadcasted_iota(jnp.int32, (dk_all, dk_all), 1) // HEAD_DIM)

    def head_stacked(x):
        return jnp.where(own_head, jnp.concatenate([x] * A_HEADS, axis=0), 0.0).astype(BF16)

    r = r_ref[...].astype(BF16)
    gk = _log_sigmoid(_dot(r, wgk_ref[...]) + bgk_ref[...]) * (1.0 / A_GATE_NORM)
    gk_hi, gk_lo = _two_bf16_terms(gk)
    k = qk_ref[:, dk_all:]
    q = qk_ref[:, :dk_all] * (HEAD_DIM ** -0.5)

    k_dec, q_st, v_st, decay = [], [], [], []
    for c in chunks:
        cum = _dot(tri2, jnp.concatenate([gk_hi[rows[c]], gk_lo[rows[c]]], axis=0))
        tot = cum[CHUNK - 1:CHUNK, :]
        k_dec.append(head_stacked(k[rows[c]] * jnp.exp(tot - cum)))
        q_st.append(head_stacked(q[rows[c]]))
        v_c = v_ref[rows[c], :]
        v_st.append(jnp.concatenate([v_c[:, h * A_DV:(h + 1) * A_DV] for h in range(A_HEADS)], axis=0))
        decay.append(jnp.exp(tot))
    kv = [_dot_tn(v_st[c], k_dec[c]) for c in chunks]

    state = state_ref[...]
    outs = []
    for c in chunks:
        state = state * decay[c] + kv[c]
        outs.append(_dot_nt(q_st[c], state.astype(BF16)))
    state_ref[...] = state

    for h in range(A_HEADS):
        cols = slice(h * A_DV, (h + 1) * A_DV)
        o = jnp.concatenate([outs[c][h * CHUNK:(h + 1) * CHUNK] for c in chunks], axis=0)
        o = o * lax.rsqrt(jnp.mean(o * o, axis=-1, keepdims=True) + EPS) * ng_ref[...]
        g = g_ref[:, cols]
        o_ref[:, cols] = (o * (g * _sigmoid(g))).astype(o_ref.dtype)


def _gla(pf, pj, wgk, bgk, ng, layer, tq):
    b, s, _ = pf.shape
    w = BRANCH_WIDTH
    return pl.pallas_call(
        functools.partial(_gla_kernel, n_chunks=tq // CHUNK),
        grid=(b, s // tq),
        in_specs=[
            pl.BlockSpec((None, tq, w), lambda bi, i: (bi, i, F_A_QK)),
            pl.BlockSpec((None, tq, w), lambda bi, i: (bi, i, QKV_A_V)),
            pl.BlockSpec((None, tq, w), lambda bi, i: (bi, i, F_A_G)),
            pl.BlockSpec((None, tq, LANES), lambda bi, i: (bi, i, F_A_R_COL // LANES)),
            pl.BlockSpec((None, LANES, A_HEADS * HEAD_DIM), lambda bi, i: (layer, 0, 0)),
            pl.BlockSpec((None, 1, A_HEADS * HEAD_DIM), lambda bi, i: (layer, 0, 0)),
            pl.BlockSpec((None, 1, A_DV), lambda bi, i: (layer, 0, 0)),
        ],
        out_specs=pl.BlockSpec((None, tq, w), lambda bi, i: (bi, i, 0)),
        out_shape=jax.ShapeDtypeStruct((b, s, w), BF16),
        scratch_shapes=[pltpu.VMEM((A_DV, A_HEADS * HEAD_DIM), F32)],
        compiler_params=_params("parallel", "arbitrary"),
        name="mixer_a_gla",
    )(pf, pj, pf, pf, wgk, bgk, ng)


def _chunk_attn_kernel(q_ref, k_ref, v_ref, ramp_ref, o_ref, qs_ref, bias_ref, kpad_ref, vfirst_ref,
                       vsecond_ref):
    step = pl.program_id(1)
    s = k_ref.shape[0]
    n_pairs = B_HEADS // 2
    init_rows = 2 * Q_BLOCK

    @pl.when(step == 0)
    def _():
        zeros = jnp.zeros((B_PAD, BRANCH_WIDTH), BF16)
        kpad_ref[0:B_PAD, :] = zeros
        vfirst_ref[0:B_PAD, :] = zeros
        vsecond_ref[0:B_PAD, :] = zeros
        kpad_ref[B_PAD:B_PAD + s, :] = k_ref[...]
        lane = lax.broadcasted_iota(jnp.int32, (init_rows, BRANCH_WIDTH), 1)
        first = (lane % LANES) < HEAD_DIM
        for r0 in range(0, s, init_rows):
            v = v_ref[r0:r0 + init_rows, :].astype(F32)
            vfirst_ref[B_PAD + r0:B_PAD + r0 + init_rows, :] = jnp.where(first, v, 0.0).astype(BF16)
            vsecond_ref[B_PAD + r0:B_PAD + r0 + init_rows, :] = jnp.where(first, 0.0, v).astype(BF16)
        qi = lax.broadcasted_iota(jnp.int32, (Q_BLOCK, BAND), 0) // CHUNK
        kj = lax.broadcasted_iota(jnp.int32, (Q_BLOCK, BAND), 1) // CHUNK
        in_band = (kj >= qi) & (kj <= qi + B_PREV_CHUNKS)
        for h in range(B_HEADS):
            ramp = jnp.broadcast_to(ramp_ref[h:h + 1, :], (Q_BLOCK, RAMP_WIDTH))
            skew = pltpu.roll(ramp, RAMP_WIDTH - (Q_BLOCK - 1), axis=1, stride=1, stride_axis=0)
            half = slice((h % 2) * Q_BLOCK, (h % 2 + 1) * Q_BLOCK)
            bias_ref[h // 2, half, :] = jnp.where(in_band, skew[:, :BAND], MASK_VALUE)

    first_head = lax.broadcasted_iota(jnp.int32, (Q_BLOCK, LANES), 1) < HEAD_DIM

    def attend(sub, has_padding):
        i = step * B_Q_BLOCKS_PER_STEP + sub
        rows = slice(sub * Q_BLOCK, (sub + 1) * Q_BLOCK)
        q = q_ref[rows, :].astype(F32) * (HEAD_DIM ** -0.5)
        for p in range(n_pairs):
            q_pair = q[:, p * LANES:(p + 1) * LANES]
            qs_ref[p, 0:Q_BLOCK, :] = jnp.where(first_head, q_pair, 0.0).astype(BF16)
            qs_ref[p, Q_BLOCK:2 * Q_BLOCK, :] = jnp.where(first_head, 0.0, q_pair).astype(BF16)
        window = pl.ds(pl.multiple_of(i * Q_BLOCK, Q_BLOCK), BAND)
        pairs = range(n_pairs)
        cols = [slice(p * LANES, (p + 1) * LANES) for p in pairs]
        sc = [_dot_nt(qs_ref[p], kpad_ref[window, cols[p]]) + bias_ref[p] for p in pairs]
        if has_padding:
            in_seq = lax.broadcasted_iota(jnp.int32, (2 * Q_BLOCK, BAND), 1) >= B_PAD - i * Q_BLOCK
            sc = [jnp.where(in_seq, x, MASK_VALUE) for x in sc]
        e = [jnp.exp(x - jnp.max(x, axis=-1, keepdims=True)) for x in sc]
        prob = [(x * (1.0 / jnp.sum(x, axis=-1, keepdims=True))).astype(BF16) for x in e]
        for p in pairs:
            both = jnp.concatenate([prob[p][:Q_BLOCK], prob[p][Q_BLOCK:]], axis=1)
            values = jnp.concatenate([vfirst_ref[window, cols[p]], vsecond_ref[window, cols[p]]], axis=0)
            o_ref[rows, cols[p]] = _dot(both, values).astype(o_ref.dtype)

    padded_steps = B_PAD // (Q_BLOCK * B_Q_BLOCKS_PER_STEP)

    @pl.when(step < padded_steps)
    def _():
        for sub in range(B_Q_BLOCKS_PER_STEP):
            attend(sub, True)

    @pl.when(step >= padded_steps)
    def _():
        for sub in range(B_Q_BLOCKS_PER_STEP):
            attend(sub, False)


def _chunk_attn(pj, ramp, layer):
    b, s, _ = pj.shape
    w = BRANCH_WIDTH
    q_rows = Q_BLOCK * B_Q_BLOCKS_PER_STEP
    return pl.pallas_call(
        _chunk_attn_kernel,
        grid=(b, s // q_rows),
        in_specs=[
            pl.BlockSpec((None, q_rows, w), lambda bi, i: (bi, i, QKV_B_Q)),
            pl.BlockSpec((None, s, w), lambda bi, i: (bi, 0, QKV_B_K)),
            pl.BlockSpec((None, s, w), lambda bi, i: (bi, 0, QKV_B_V)),
            pl.BlockSpec((None, B_HEADS, RAMP_WIDTH), lambda bi, i: (layer, 0, 0)),
        ],
        out_specs=pl.BlockSpec((None, q_rows, w), lambda bi, i: (bi, i, 0)),
        out_shape=jax.ShapeDtypeStruct((b, s, w), BF16),
        scratch_shapes=[
            pltpu.VMEM((B_HEADS // 2, 2 * Q_BLOCK, LANES), BF16),
            pltpu.VMEM((B_HEADS // 2, 2 * Q_BLOCK, BAND), F32),
            pltpu.VMEM((s + B_PAD, w), BF16),
            pltpu.VMEM((s + B_PAD, w), BF16),
            pltpu.VMEM((s + B_PAD, w), BF16),
        ],
        compiler_params=_params("parallel", "arbitrary"),
        name="mixer_b_chunk_attn",
    )(pj, pj, pj, ramp)


def _bias_ramp(rel_table):
    idx = jnp.clip((BAND - 1) - jnp.arange(RAMP_WIDTH), -B_MAX_REL, B_MAX_REL) + B_MAX_REL
    return rel_table.astype(F32)[:, :, idx]


def _rglru_kernel(g_ref, x_ref, cw_ref, cb_ref, wa_ref, ba_ref, wx_ref, bx_ref, lam_ref, o_ref,
                  xe_ref, a_ref, b_ref, h_ref, carry_ref):
    ts = x_ref.shape[0]
    halo = SUBLANES

    @pl.when(pl.program_id(1) == 0)
    def _():
        xe_ref[0:halo, :] = jnp.zeros((halo, BRANCH_WIDTH), F32)
        carry_ref[...] = jnp.zeros_like(carry_ref)

    x = x_ref[...]
    xe_ref[halo:halo + ts, :] = x
    xc = cb_ref[...]
    for j in range(C_CONV - 1):
        lag = C_CONV - 1 - j
        xc = xc + xe_ref[halo - lag:halo - lag + ts, :] * cw_ref[j:j + 1, :]
    xc = xc + x * cw_ref[C_CONV - 1:C_CONV, :]
    xe_ref[0:halo, :] = xe_ref[ts:ts + halo, :]

    xcb = xc.astype(BF16)
    r = _sigmoid(_dot(xcb, wa_ref[...]) + ba_ref[...])
    gate_i = _sigmoid(_dot(xcb, wx_ref[...]) + bx_ref[...])
    lam = lam_ref[...]
    softplus_neg_lam = jnp.maximum(-lam, 0.0) + _log1p_exp_neg_abs(lam)
    log_a = -C_POW * r * softplus_neg_lam
    a = jnp.exp(log_a)
    a_ref[...] = a
    b_ref[...] = jnp.sqrt(-jnp.tanh(log_a) * (a * a + 1.0)) * (gate_i * xc)

    sub = lax.broadcasted_iota(jnp.int32, (SUBLANES, BRANCH_WIDTH), 0)

    def group(gi, carry):
        rows = pl.ds(pl.multiple_of(gi * SUBLANES, SUBLANES), SUBLANES)
        a = a_ref[rows, :]
        bb = b_ref[rows, :]
        for d in (1, 2, 4):
            a_prev = pltpu.roll(a, d, axis=0)
            b_prev = pltpu.roll(bb, d, axis=0)
            has_prev = sub >= d
            bb = jnp.where(has_prev, a * b_prev + bb, bb)
            a = jnp.where(has_prev, a * a_prev, a)
        h = a * carry + bb
        h_ref[rows, :] = h
        return jnp.broadcast_to(h[SUBLANES - 1:SUBLANES, :], (SUBLANES, BRANCH_WIDTH))

    carry_ref[...] = lax.fori_loop(0, ts // SUBLANES, group, carry_ref[...], unroll=8)
    o_ref[...] = (jax.nn.gelu(g_ref[...]) * h_ref[...]).astype(o_ref.dtype)


def _rglru(pf, cw, cb, wa, ba, wx, bx, lam, layer, ts):
    b, s, _ = pf.shape
    w = BRANCH_WIDTH
    vec = pl.BlockSpec((None, 1, w), lambda bi, i: (layer, 0, 0))
    mat = pl.BlockSpec((None, w, w), lambda bi, i: (layer, 0, 0))
    return pl.pallas_call(
        _rglru_kernel,
        grid=(b, s // ts),
        in_specs=[
            pl.BlockSpec((None, ts, w), lambda bi, i: (bi, i, F_C_G)),
            pl.BlockSpec((None, ts, w), lambda bi, i: (bi, i, F_C_X)),
            pl.BlockSpec((None, C_CONV, w), lambda bi, i: (layer, 0, 0)),
            vec, mat, vec, mat, vec, vec,
        ],
        out_specs=pl.BlockSpec((None, ts, w), lambda bi, i: (bi, i, 0)),
        out_shape=jax.ShapeDtypeStruct((b, s, w), BF16),
        scratch_shapes=[
            pltpu.VMEM((ts + 2 * SUBLANES, w), F32),
            pltpu.VMEM((ts, w), F32),
            pltpu.VMEM((ts, w), F32),
            pltpu.VMEM((ts, w), F32),
            pltpu.VMEM((SUBLANES, w), F32),
        ],
        compiler_params=_params("parallel", "arbitrary"),
        name="mixer_c_rglru",
    )(pf, pf, cw, cb, wa, ba, wx, bx, lam)


def _block_diag(wb):
    l, g, n, _ = wb.shape
    eye = jnp.eye(g, dtype=wb.dtype)
    return (wb[:, :, :, None, :] * eye[None, :, None, :, None]).reshape(l, g * n, g * n)


def _stick_kernel(q_ref, k_ref, v_ref, o_ref, qs_ref, vs_ref, later_ref, acc_ref):
    step = pl.program_id(1)
    n_pairs = D_HEADS // 2
    n_key_blocks = k_ref.shape[0] // Q_BLOCK
    first_head = lax.broadcasted_iota(jnp.int32, (Q_BLOCK, LANES), 1) < HEAD_DIM

    @pl.when(step == 0)
    def _():
        for jb in range(n_key_blocks):
            for p in range(n_pairs):
                v = v_ref[jb * Q_BLOCK:(jb + 1) * Q_BLOCK, p * LANES:(p + 1) * LANES].astype(F32)
                vs_ref[jb, p, 0:Q_BLOCK, :] = jnp.where(first_head, v, 0.0).astype(BF16)
                vs_ref[jb, p, Q_BLOCK:2 * Q_BLOCK, :] = jnp.where(first_head, 0.0, v).astype(BF16)

    row = lax.broadcasted_iota(jnp.int32, (2 * Q_BLOCK, Q_BLOCK), 0) % Q_BLOCK
    col = lax.broadcasted_iota(jnp.int32, (2 * Q_BLOCK, Q_BLOCK), 1)
    strictly_before = col < row
    rr = lax.broadcasted_iota(jnp.int32, (2 * Q_BLOCK, 2 * Q_BLOCK), 0) % Q_BLOCK
    cc = lax.broadcasted_iota(jnp.int32, (2 * Q_BLOCK, 2 * Q_BLOCK), 1)
    suffix_and_total = jnp.where((rr > cc) | (cc >= Q_BLOCK), 1.0, 0.0).astype(BF16)

    def key_blocks(js, from_diagonal):
        pairs = range(n_pairs)
        blocks = range(len(js))
        ks = [pl.multiple_of(j * Q_BLOCK, Q_BLOCK) for j in js]
        both = [(b, p) for b in blocks for p in pairs]
        z, neg_log1m, log_sig, sums, att = {}, {}, {}, {}, {}
        for b, p in both:
            z[b, p] = _dot_nt(qs_ref[p], k_ref[pl.ds(ks[b], Q_BLOCK), p * LANES:(p + 1) * LANES])
        for b, p in both:
            t = jnp.maximum(z[b, p], 0.0) + jnp.log(1.0 + jnp.exp(-jnp.abs(z[b, p])))
            log_sig[b, p] = z[b, p] - t
            masked = from_diagonal and b == 0
            neg_log1m[b, p] = jnp.where(strictly_before, t, 0.0) if masked else t
        for b, p in both:
            hi, lo = _two_bf16_terms(neg_log1m[b, p])
            sums[b, p] = _dot(jnp.concatenate([hi, lo], axis=1), suffix_and_total)
        tails, totals = {}, []
        for p in pairs:
            later = None if from_diagonal else later_ref[p]
            for b in blocks:
                tail = sums[b, p][:, :Q_BLOCK]
                total = sums[b, p][:, Q_BLOCK:]
                if later is not None:
                    tail = tail + later
                    total = total + later
                tails[b, p] = tail
                later = total
            later_ref[p] = later
            totals.append(later)
        least = jnp.min(jnp.minimum(jnp.minimum(totals[0], totals[1]), jnp.minimum(totals[2], totals[3])))
        for b, p in both:
            a = jnp.exp(log_sig[b, p] - tails[b, p])
            if from_diagonal and b == 0:
                a = jnp.where(strictly_before, a, 0.0)
            a = a.astype(BF16)
            att[b, p] = jnp.concatenate([a[:Q_BLOCK], a[Q_BLOCK:]], axis=1)
        for p in pairs:
            pv = None if from_diagonal else acc_ref[p]
            for b in blocks:
                term = _dot(att[b, p], vs_ref[js[b], p])
                pv = term if pv is None else pv + term
            acc_ref[p] = pv
        return least

    def query_block(sub):
        i = step * D_Q_BLOCKS_PER_STEP + sub
        rows = slice(sub * Q_BLOCK, (sub + 1) * Q_BLOCK)
        q = q_ref[rows, :].astype(F32) * (HEAD_DIM ** -0.5)
        for p in range(n_pairs):
            q_pair = q[:, p * LANES:(p + 1) * LANES]
            qs_ref[p, 0:Q_BLOCK, :] = jnp.where(first_head, q_pair, 0.0).astype(BF16)
            qs_ref[p, Q_BLOCK:2 * Q_BLOCK, :] = jnp.where(first_head, 0.0, q_pair).astype(BF16)

        least = lax.cond(
            i > 1, lambda: key_blocks([i - b for b in range(ALWAYS_VISITED)], True),
            lambda: lax.cond(i > 0, lambda: key_blocks([i, i - 1], True), lambda: key_blocks([i], True)))

        def more(carry):
            jj, least = carry
            return jnp.logical_and(jj < i, least < EXP_UNDERFLOW)

        def earlier(carry):
            jj, _ = carry
            return jj + 1, key_blocks([i - 1 - jj], False)

        lax.while_loop(more, earlier, (jnp.int32(ALWAYS_VISITED - 1), least))

        for p in range(n_pairs):
            o_ref[rows, p * LANES:(p + 1) * LANES] = acc_ref[p].astype(o_ref.dtype)

    for sub in range(D_Q_BLOCKS_PER_STEP):
        query_block(sub)


def _stick(pj):
    b, s, _ = pj.shape
    w = BRANCH_WIDTH
    q_rows = Q_BLOCK * D_Q_BLOCKS_PER_STEP
    return pl.pallas_call(
        _stick_kernel,
        grid=(b, s // q_rows),
        in_specs=[
            pl.BlockSpec((None, q_rows, w), lambda bi, i: (bi, i, QKV_D_Q)),
            pl.BlockSpec((None, s, w), lambda bi, i: (bi, 0, QKV_D_K)),
            pl.BlockSpec((None, s, w), lambda bi, i: (bi, 0, QKV_D_V)),
        ],
        out_specs=pl.BlockSpec((None, q_rows, w), lambda bi, i: (bi, i, 0)),
        out_shape=jax.ShapeDtypeStruct((b, s, w), BF16),
        scratch_shapes=[
            pltpu.VMEM((D_HEADS // 2, 2 * Q_BLOCK, LANES), BF16),
            pltpu.VMEM((s // Q_BLOCK, D_HEADS // 2, 2 * Q_BLOCK, LANES), BF16),
            pltpu.VMEM((D_HEADS // 2, 2 * Q_BLOCK, LANES), F32),
            pltpu.VMEM((D_HEADS // 2, Q_BLOCK, LANES), F32),
        ],
        compiler_params=_params("parallel", "arbitrary"),
        name="mixer_d_stick",
    )(pj, pj, pj)


def _merge_kernel(ya_ref, yb_ref, yc_ref, yd_ref, xn_ref, h_ref, wgate_ref, wb_ref, wo_ref, o_ref):
    d = h_ref.shape[1]
    xn = xn_ref[...]
    mixed = None
    for n, y_ref in enumerate((ya_ref, yb_ref, yc_ref, yd_ref)):
        gate = _sigmoid(_dot(xn, wgate_ref[:, n * d:(n + 1) * d]))
        term = gate * _dot(y_ref[...], wb_ref[n])
        mixed = term if mixed is None else mixed + term
    o_ref[...] = h_ref[...] + _dot(mixed.astype(BF16), wo_ref[...])


def _merge(ya, yb, yc, yd, xn, h, wgate, wb, wo, layer, tm):
    t, d = h.shape
    w = BRANCH_WIDTH
    y_spec = pl.BlockSpec((tm, w), lambda i: (i, 0))
    row = pl.BlockSpec((tm, d), lambda i: (i, 0))
    resident = functools.partial(pl.BlockSpec, pipeline_mode=pl.Buffered(1))
    return pl.pallas_call(
        _merge_kernel,
        grid=(t // tm,),
        in_specs=[
            y_spec, y_spec, y_spec, y_spec, row, row,
            resident((None, d, N_BRANCH * d), lambda i: (layer, 0, 0)),
            resident((None, N_BRANCH, w, d), lambda i: (layer, 0, 0, 0)),
            resident((None, d, d), lambda i: (layer, 0, 0)),
        ],
        out_specs=row,
        out_shape=jax.ShapeDtypeStruct((t, d), F32),
        compiler_params=_params("parallel"),
        name="merge",
    )(ya, yb, yc, yd, xn, h, wgate, wb, wo)


def _ffn_kernel(h_ref, g_ref, wg_ref, wu_ref, wd_ref, gnext_ref, *out_refs, hidden_chunk, last):
    h = h_ref[...]
    hn = _rmsnorm(h, g_ref[...]).astype(BF16)
    hidden = wg_ref.shape[1]
    acc = h
    for c0 in range(0, hidden, hidden_chunk):
        cs = slice(c0, min(c0 + hidden_chunk, hidden))
        gate = _dot(hn, wg_ref[:, cs])
        up = _dot(hn, wu_ref[:, cs])
        act = (gate * _sigmoid(gate) * up).astype(BF16)
        acc = acc + _dot(act, wd_ref[cs, :])
    normed = _rmsnorm(acc, gnext_ref[...])
    if last:
        out_refs[0][...] = normed
    else:
        out_refs[0][...] = acc
        out_refs[1][...] = normed.astype(BF16)


def _ffn(h, g, wg, wu, wd, gnext, layer, next_layer, tm):
    t, d = h.shape
    hidden = wg.shape[-1]
    last = next_layer is None
    row = pl.BlockSpec((tm, d), lambda i: (i, 0))
    resident = functools.partial(pl.BlockSpec, pipeline_mode=pl.Buffered(1))
    if last:
        out_specs, out_shape = row, jax.ShapeDtypeStruct((t, d), F32)
    else:
        out_specs = (row, row)
        out_shape = (jax.ShapeDtypeStruct((t, d), F32), jax.ShapeDtypeStruct((t, d), BF16))
    gnext_row = 0 if last else next_layer
    return pl.pallas_call(
        functools.partial(_ffn_kernel, hidden_chunk=512, last=last),
        grid=(t // tm,),
        in_specs=[
            row,
            pl.BlockSpec((None, 1, d), lambda i: (layer, 0, 0)),
            resident((None, d, hidden), lambda i: (layer, 0, 0)),
            resident((None, d, hidden), lambda i: (layer, 0, 0)),
            resident((None, hidden, d), lambda i: (layer, 0, 0)),
            pl.BlockSpec((None, 1, d), lambda i: (gnext_row, 0, 0)),
        ],
        out_specs=out_specs,
        out_shape=out_shape,
        compiler_params=_params("parallel"),
        name="ffn",
    )(h, g, wg, wu, wd, gnext)


def kernel(x, norm_mix, w_in, a_w_gk, a_b_gk, a_norm, b_rel_bias, c_conv_w, c_conv_b, c_w_a, c_b_a,
           c_w_x, c_b_x, c_lambda, w_branch, w_out, norm_ffn, w_ffn_gate, w_ffn_up, w_ffn_down,
           norm_final):
    bsz, s, d = x.shape
    depth = w_in.shape[0]
    t = bsz * s
    w = BRANCH_WIDTH

    sizes = (A_HEADS * HEAD_DIM, A_HEADS * HEAD_DIM, A_HEADS * A_DV, A_RANK, A_HEADS * A_DV,
             w, w, w, w, w, w, w, w, N_BRANCH * d)
    offs = [0]
    for sz in sizes:
        offs.append(offs[-1] + sz)
    (o_aq, o_ak, o_av, o_ar, o_ag, o_bq, _, _, o_cg, _, o_dq, _, _, o_gate, o_end) = offs

    w_in_b = w_in.astype(BF16)
    w_qkv = jnp.concatenate(
        [w_in_b[:, :, o_av:o_ar], w_in_b[:, :, o_bq:o_cg], w_in_b[:, :, o_dq:o_gate]], axis=-1)
    w_f = jnp.concatenate(
        [w_in_b[:, :, o_ag:o_bq], w_in_b[:, :, o_cg:o_dq], w_in_b[:, :, o_aq:o_av],
         w_in_b[:, :, o_ar:o_ag], jnp.zeros((depth, d, 2 * LANES - A_RANK), BF16)], axis=-1)
    w_gate = w_in_b[:, :, o_gate:o_end]
    wgk = jnp.pad(a_w_gk, ((0, 0), (0, LANES - A_RANK), (0, 0))).astype(BF16)
    vec = lambda p: p.reshape(depth, 1, -1)
    ramp = _bias_ramp(b_rel_bias)
    wa = _block_diag(c_w_a).astype(BF16)
    wx = _block_diag(c_w_x).astype(BF16)
    wb = w_branch.astype(BF16)
    wo = w_out.astype(BF16)
    wg = w_ffn_gate.astype(BF16)
    wu = w_ffn_up.astype(BF16)
    wd = w_ffn_down.astype(BF16)
    gf = norm_final.reshape(1, 1, d)

    h = x.reshape(t, d)
    tm = min(1024, t)
    ts = 512
    xn = _norm(h, vec(norm_mix), 0, tm)
    for layer in range(depth):
        pj = _proj(xn, w_qkv, layer, BF16, tm, QKV_COLS).reshape(bsz, s, QKV_COLS)
        pf = _proj(xn, w_f, layer, F32, tm, F_COLS).reshape(bsz, s, F_COLS)
        ya = _gla(pf, pj, wgk, vec(a_b_gk), vec(a_norm), layer, ts)
        yb = _chunk_attn(pj, ramp, layer)
        yc = _rglru(pf, c_conv_w, vec(c_conv_b), wa, vec(c_b_a), wx, vec(c_b_x), vec(c_lambda),
                    layer, ts)
        yd = _stick(pj)
        h = _merge(ya.reshape(t, w), yb.reshape(t, w), yc.reshape(t, w), yd.reshape(t, w),
                   xn, h, w_gate, wb, wo, layer, tm)
        if layer + 1 < depth:
            h, xn = _ffn(h, vec(norm_ffn), wg, wu, wd, vec(norm_mix), layer, layer + 1, tm)
        else:
            h = _ffn(h, vec(norm_ffn), wg, wu, wd, gf, layer, None, tm)
    return h.reshape(bsz, s, d)
```

```python
import functools

import jax
import jax.numpy as jnp
from jax import lax
from jax.experimental import pallas as pl
from jax.experimental.pallas import tpu as pltpu

F32 = jnp.float32
BF16 = jnp.bfloat16

EPS = 1e-6
CHUNK = 64
BRANCH_WIDTH = 512
N_BRANCH = 4
HEAD_DIM = 64
A_HEADS = 4
A_DV = 128
A_RANK = 16
A_GATE_NORM = 16.0
B_HEADS = 8
B_PREV_CHUNKS = 8
B_MAX_REL = 128
C_BLOCKS = 8
C_CONV = 4
C_POW = 8.0
D_HEADS = 8
MASK_VALUE = -1e30
EXP_UNDERFLOW = 104.0
ALWAYS_VISITED = 3

LANES = 128
SUBLANES = 8
VMEM_BYTES_V7X = 64 * 1024 * 1024

QKV_COLS = 7 * BRANCH_WIDTH
QKV_A_V, QKV_B_Q, QKV_B_K, QKV_B_V, QKV_D_Q, QKV_D_K, QKV_D_V = range(7)
F_A_G, F_C_G, F_C_X, F_A_QK = range(4)
F_A_R_COL = 4 * BRANCH_WIDTH
F_COLS = F_A_R_COL + 2 * LANES

Q_BLOCK = 128
B_Q_BLOCKS_PER_STEP = 4
D_Q_BLOCKS_PER_STEP = 2
BAND = (B_PREV_CHUNKS + 2) * CHUNK
B_PAD = B_PREV_CHUNKS * CHUNK
RAMP_WIDTH = BAND + Q_BLOCK


def _dot(a, b):
    return jnp.dot(a, b, preferred_element_type=F32)


def _dot_nt(a, b):
    return lax.dot_general(a, b, (((1,), (1,)), ((), ())), preferred_element_type=F32)


def _dot_tn(a, b):
    return lax.dot_general(a, b, (((0,), (0,)), ((), ())), preferred_element_type=F32)


def _two_bf16_terms(x):
    hi = x.astype(BF16)
    lo = (x - hi.astype(F32)).astype(BF16)
    return hi, lo


def _log1p_exp_neg_abs(z):
    return jnp.log1p(jnp.exp(-jnp.abs(z)))


def _log_sigmoid(z):
    return jnp.minimum(z, 0.0) - _log1p_exp_neg_abs(z)


def _sigmoid(z):
    return 1.0 / (1.0 + jnp.exp(-z))


VMEM_REQUEST_BYTES = VMEM_BYTES_V7X - 8 * 1024 * 1024


def _params(*semantics):
    return pltpu.CompilerParams(dimension_semantics=semantics, vmem_limit_bytes=VMEM_REQUEST_BYTES)


def _rmsnorm(x, g):
    return x * lax.rsqrt(jnp.mean(x * x, axis=-1, keepdims=True) + EPS) * g


def _norm_kernel(x_ref, g_ref, o_ref):
    o_ref[...] = _rmsnorm(x_ref[...], g_ref[...]).astype(o_ref.dtype)


def _norm(x, g, layer, tm):
    t, d = x.shape
    return pl.pallas_call(
        _norm_kernel,
        grid=(t // tm,),
        in_specs=[
            pl.BlockSpec((tm, d), lambda i: (i, 0)),
            pl.BlockSpec((None, 1, d), lambda i: (layer, 0, 0)),
        ],
        out_specs=pl.BlockSpec((tm, d), lambda i: (i, 0)),
        out_shape=jax.ShapeDtypeStruct((t, d), BF16),
        compiler_params=_params("parallel"),
        name="norm",
    )(x, g)


def _proj_kernel(x_ref, w_ref, o_ref):
    o_ref[...] = _dot(x_ref[...], w_ref[...]).astype(o_ref.dtype)


def _proj(xn, w, layer, out_dtype, tm, tn):
    t, d = xn.shape
    n = w.shape[-1]
    return pl.pallas_call(
        _proj_kernel,
        grid=(t // tm, n // tn),
        in_specs=[
            pl.BlockSpec((tm, d), lambda i, j: (i, 0)),
            pl.BlockSpec((None, d, tn), lambda i, j: (layer, 0, j)),
        ],
        out_specs=pl.BlockSpec((tm, tn), lambda i, j: (i, j)),
        out_shape=jax.ShapeDtypeStruct((t, n), out_dtype),
        compiler_params=_params("parallel", "arbitrary"),
        name="proj",
    )(xn, w)


def _gla_kernel(qk_ref, v_ref, g_ref, r_ref, wgk_ref, bgk_ref, ng_ref, o_ref, state_ref, *,
                n_chunks):
    dk_all = A_HEADS * HEAD_DIM
    chunks = range(n_chunks)
    rows = [slice(c * CHUNK, (c + 1) * CHUNK) for c in chunks]

    @pl.when(pl.program_id(1) == 0)
    def _():
        state_ref[...] = jnp.zeros_like(state_ref)

    tri2 = jnp.where(lax.broadcasted_iota(jnp.int32, (CHUNK, 2 * CHUNK), 0)
                     >= lax.broadcasted_iota(jnp.int32, (CHUNK, 2 * CHUNK), 1) % CHUNK,
                     1.0, 0.0).astype(BF16)
    own_head = (lax.broadcasted_iota(jnp.int32, (dk_all, dk_all), 0) // HEAD_DIM
                == lax.broadcasted_iota(jnp.int32, (dk_all, dk_all), 1) // HEAD_DIM)

    def head_stacked(x):
        return jnp.where(own_head, jnp.concatenate([x] * A_HEADS, axis=0), 0.0).astype(BF16)

    r = r_ref[...].astype(BF16)
    gk = _log_sigmoid(_dot(r, wgk_ref[...]) + bgk_ref[...]) * (1.0 / A_GATE_NORM)
    gk_hi, gk_lo = _two_bf16_terms(gk)
    k = qk_ref[:, dk_all:]
    q = qk_ref[:, :dk_all] * (HEAD_DIM ** -0.5)

    k_dec, q_st, v_st, decay = [], [], [], []
    for c in chunks:
        cum = _dot(tri2, jnp.concatenate([gk_hi[rows[c]], gk_lo[rows[c]]], axis=0))
        tot = cum[CHUNK - 1:CHUNK, :]
        k_dec.append(head_stacked(k[rows[c]] * jnp.exp(tot - cum)))
        q_st.append(head_stacked(q[rows[c]]))
        v_c = v_ref[rows[c], :]
        v_st.append(jnp.concatenate([v_c[:, h * A_DV:(h + 1) * A_DV] for h in range(A_HEADS)], axis=0))
        decay.append(jnp.exp(tot))
    kv = [_dot_tn(v_st[c], k_dec[c]) for c in chunks]

    state = state_ref[...]
    outs = []
    for c in chunks:
        state = state * decay[c] + kv[c]
        outs.append(_dot_nt(q_st[c], state.astype(BF16)))
    state_ref[...] = state

    for h in range(A_HEADS):
        cols = slice(h * A_DV, (h + 1) * A_DV)
        o = jnp.concatenate([outs[c][h * CHUNK:(h + 1) * CHUNK] for c in chunks], axis=0)
        o = o * lax.rsqrt(jnp.mean(o * o, axis=-1, keepdims=True) + EPS) * ng_ref[...]
        g = g_ref[:, cols]
        o_ref[:, cols] = (o * (g * _sigmoid(g))).astype(o_ref.dtype)


def _gla(pf, pj, wgk, bgk, ng, layer, tq):
    b, s, _ = pf.shape
    w = BRANCH_WIDTH
    return pl.pallas_call(
        functools.partial(_gla_kernel, n_chunks=tq // CHUNK),
        grid=(b, s // tq),
        in_specs=[
            pl.BlockSpec((None, tq, w), lambda bi, i: (bi, i, F_A_QK)),
            pl.BlockSpec((None, tq, w), lambda bi, i: (bi, i, QKV_A_V)),
            pl.BlockSpec((None, tq, w), lambda bi, i: (bi, i, F_A_G)),
            pl.BlockSpec((None, tq, LANES), lambda bi, i: (bi, i, F_A_R_COL // LANES)),
            pl.BlockSpec((None, LANES, A_HEADS * HEAD_DIM), lambda bi, i: (layer, 0, 0)),
            pl.BlockSpec((None, 1, A_HEADS * HEAD_DIM), lambda bi, i: (layer, 0, 0)),
            pl.BlockSpec((None, 1, A_DV), lambda bi, i: (layer, 0, 0)),
        ],
        out_specs=pl.BlockSpec((None, tq, w), lambda bi, i: (bi, i, 0)),
        out_shape=jax.ShapeDtypeStruct((b, s, w), BF16),
        scratch_shapes=[pltpu.VMEM((A_DV, A_HEADS * HEAD_DIM), F32)],
        compiler_params=_params("parallel", "arbitrary"),
        name="mixer_a_gla",
    )(pf, pj, pf, pf, wgk, bgk, ng)


def _chunk_attn_kernel(q_ref, k_ref, v_ref, ramp_ref, o_ref, qs_ref, bias_ref, kpad_ref, vfirst_ref,
                       vsecond_ref):
    step = pl.program_id(1)
    s = k_ref.shape[0]
    n_pairs = B_HEADS // 2
    init_rows = 2 * Q_BLOCK

    @pl.when(step == 0)
    def _():
        zeros = jnp.zeros((B_PAD, BRANCH_WIDTH), BF16)
        kpad_ref[0:B_PAD, :] = zeros
        vfirst_ref[0:B_PAD, :] = zeros
        vsecond_ref[0:B_PAD, :] = zeros
        kpad_ref[B_PAD:B_PAD + s, :] = k_ref[...]
        lane = lax.broadcasted_iota(jnp.int32, (init_rows, BRANCH_WIDTH), 1)
        first = (lane % LANES) < HEAD_DIM
        for r0 in range(0, s, init_rows):
            v = v_ref[r0:r0 + init_rows, :].astype(F32)
            vfirst_ref[B_PAD + r0:B_PAD + r0 + init_rows, :] = jnp.where(first, v, 0.0).astype(BF16)
            vsecond_ref[B_PAD + r0:B_PAD + r0 + init_rows, :] = jnp.where(first, 0.0, v).astype(BF16)
        qi = lax.broadcasted_iota(jnp.int32, (Q_BLOCK, BAND), 0) // CHUNK
        kj = lax.broadcasted_iota(jnp.int32, (Q_BLOCK, BAND), 1) // CHUNK
        in_band = (kj >= qi) & (kj <= qi + B_PREV_CHUNKS)
        for h in range(B_HEADS):
            ramp = jnp.broadcast_to(ramp_ref[h:h + 1, :], (Q_BLOCK, RAMP_WIDTH))
            skew = pltpu.roll(ramp, RAMP_WIDTH - (Q_BLOCK - 1), axis=1, stride=1, stride_axis=0)
            half = slice((h % 2) * Q_BLOCK, (h % 2 + 1) * Q_BLOCK)
            bias_ref[h // 2, half, :] = jnp.where(in_band, skew[:, :BAND], MASK_VALUE)

    first_head = lax.broadcasted_iota(jnp.int32, (Q_BLOCK, LANES), 1) < HEAD_DIM

    def attend(sub, has_padding):
        i = step * B_Q_BLOCKS_PER_STEP + sub
        rows = slice(sub * Q_BLOCK, (sub + 1) * Q_BLOCK)
        q = q_ref[rows, :].astype(F32) * (HEAD_DIM ** -0.5)
        for p in range(n_pairs):
            q_pair = q[:, p * LANES:(p + 1) * LANES]
            qs_ref[p, 0:Q_BLOCK, :] = jnp.where(first_head, q_pair, 0.0).astype(BF16)
            qs_ref[p, Q_BLOCK:2 * Q_BLOCK, :] = jnp.where(first_head, 0.0, q_pair).astype(BF16)
        window = pl.ds(pl.multiple_of(i * Q_BLOCK, Q_BLOCK), BAND)
        pairs = range(n_pairs)
        cols = [slice(p * LANES, (p + 1) * LANES) for p in pairs]
        sc = [_dot_nt(qs_ref[p], kpad_ref[window, cols[p]]) + bias_ref[p] for p in pairs]
        if has_padding:
            in_seq = lax.broadcasted_iota(jnp.int32, (2 * Q_BLOCK, BAND), 1) >= B_PAD - i * Q_BLOCK
            sc = [jnp.where(in_seq, x, MASK_VALUE) for x in sc]
        e = [jnp.exp(x - jnp.max(x, axis=-1, keepdims=True)) for x in sc]
        prob = [(x * (1.0 / jnp.sum(x, axis=-1, keepdims=True))).astype(BF16) for x in e]
        for p in pairs:
            both = jnp.concatenate([prob[p][:Q_BLOCK], prob[p][Q_BLOCK:]], axis=1)
            values = jnp.concatenate([vfirst_ref[window, cols[p]], vsecond_ref[window, cols[p]]], axis=0)
            o_ref[rows, cols[p]] = _dot(both, values).astype(o_ref.dtype)

    padded_steps = B_PAD // (Q_BLOCK * B_Q_BLOCKS_PER_STEP)

    @pl.when(step < padded_steps)
    def _():
        for sub in range(B_Q_BLOCKS_PER_STEP):
            attend(sub, True)

    @pl.when(step >= padded_steps)
    def _():
        for sub in range(B_Q_BLOCKS_PER_STEP):
            attend(sub, False)


def _chunk_attn(pj, ramp, layer):
    b, s, _ = pj.shape
    w = BRANCH_WIDTH
    q_rows = Q_BLOCK * B_Q_BLOCKS_PER_STEP
    return pl.pallas_call(
        _chunk_attn_kernel,
        grid=(b, s // q_rows),
        in_specs=[
            pl.BlockSpec((None, q_rows, w), lambda bi, i: (bi, i, QKV_B_Q)),
            pl.BlockSpec((None, s, w), lambda bi, i: (bi, 0, QKV_B_K)),
            pl.BlockSpec((None, s, w), lambda bi, i: (bi, 0, QKV_B_V)),
            pl.BlockSpec((None, B_HEADS, RAMP_WIDTH), lambda bi, i: (layer, 0, 0)),
        ],
        out_specs=pl.BlockSpec((None, q_rows, w), lambda bi, i: (bi, i, 0)),
        out_shape=jax.ShapeDtypeStruct((b, s, w), BF16),
        scratch_shapes=[
            pltpu.VMEM((B_HEADS // 2, 2 * Q_BLOCK, LANES), BF16),
            pltpu.VMEM((B_HEADS // 2, 2 * Q_BLOCK, BAND), F32),
            pltpu.VMEM((s + B_PAD, w), BF16),
            pltpu.VMEM((s + B_PAD, w), BF16),
            pltpu.VMEM((s + B_PAD, w), BF16),
        ],
        compiler_params=_params("parallel", "arbitrary"),
        name="mixer_b_chunk_attn",
    )(pj, pj, pj, ramp)


def _bias_ramp(rel_table):
    idx = jnp.clip((BAND - 1) - jnp.arange(RAMP_WIDTH), -B_MAX_REL, B_MAX_REL) + B_MAX_REL
    return rel_table.astype(F32)[:, :, idx]


def _rglru_kernel(g_ref, x_ref, cw_ref, cb_ref, wa_ref, ba_ref, wx_ref, bx_ref, lam_ref, o_ref,
                  xe_ref, a_ref, b_ref, h_ref, carry_ref):
    ts = x_ref.shape[0]
    halo = SUBLANES

    @pl.when(pl.program_id(1) == 0)
    def _():
        xe_ref[0:halo, :] = jnp.zeros((halo, BRANCH_WIDTH), F32)
        carry_ref[...] = jnp.zeros_like(carry_ref)

    x = x_ref[...]
    xe_ref[halo:halo + ts, :] = x
    xc = cb_ref[...]
    for j in range(C_CONV - 1):
        lag = C_CONV - 1 - j
        xc = xc + xe_ref[halo - lag:halo - lag + ts, :] * cw_ref[j:j + 1, :]
    xc = xc + x * cw_ref[C_CONV - 1:C_CONV, :]
    xe_ref[0:halo, :] = xe_ref[ts:ts + halo, :]

    xcb = xc.astype(BF16)
    r = _sigmoid(_dot(xcb, wa_ref[...]) + ba_ref[...])
    gate_i = _sigmoid(_dot(xcb, wx_ref[...]) + bx_ref[...])
    lam = lam_ref[...]
    softplus_neg_lam = jnp.maximum(-lam, 0.0) + _log1p_exp_neg_abs(lam)
    log_a = -C_POW * r * softplus_neg_lam
    a = jnp.exp(log_a)
    a_ref[...] = a
    b_ref[...] = jnp.sqrt(-jnp.tanh(log_a) * (a * a + 1.0)) * (gate_i * xc)

    sub = lax.broadcasted_iota(jnp.int32, (SUBLANES, BRANCH_WIDTH), 0)

    def group(gi, carry):
        rows = pl.ds(pl.multiple_of(gi * SUBLANES, SUBLANES), SUBLANES)
        a = a_ref[rows, :]
        bb = b_ref[rows, :]
        for d in (1, 2, 4):
            a_prev = pltpu.roll(a, d, axis=0)
            b_prev = pltpu.roll(bb, d, axis=0)
            has_prev = sub >= d
            bb = jnp.where(has_prev, a * b_prev + bb, bb)
            a = jnp.where(has_prev, a * a_prev, a)
        h = a * carry + bb
        h_ref[rows, :] = h
        return jnp.broadcast_to(h[SUBLANES - 1:SUBLANES, :], (SUBLANES, BRANCH_WIDTH))

    carry_ref[...] = lax.fori_loop(0, ts // SUBLANES, group, carry_ref[...], unroll=8)
    o_ref[...] = (jax.nn.gelu(g_ref[...]) * h_ref[...]).astype(o_ref.dtype)


def _rglru(pf, cw, cb, wa, ba, wx, bx, lam, layer, ts):
    b, s, _ = pf.shape
    w = BRANCH_WIDTH
    vec = pl.BlockSpec((None, 1, w), lambda bi, i: (layer, 0, 0))
    mat = pl.BlockSpec((None, w, w), lambda bi, i: (layer, 0, 0))
    return pl.pallas_call(
        _rglru_kernel,
        grid=(b, s // ts),
        in_specs=[
            pl.BlockSpec((None, ts, w), lambda bi, i: (bi, i, F_C_G)),
            pl.BlockSpec((None, ts, w), lambda bi, i: (bi, i, F_C_X)),
            pl.BlockSpec((None, C_CONV, w), lambda bi, i: (layer, 0, 0)),
            vec, mat, vec, mat, vec, vec,
        ],
        out_specs=pl.BlockSpec((None, ts, w), lambda bi, i: (bi, i, 0)),
        out_shape=jax.ShapeDtypeStruct((b, s, w), BF16),
        scratch_shapes=[
            pltpu.VMEM((ts + 2 * SUBLANES, w), F32),
            pltpu.VMEM((ts, w), F32),
            pltpu.VMEM((ts, w), F32),
            pltpu.VMEM((ts, w), F32),
            pltpu.VMEM((SUBLANES, w), F32),
        ],
        compiler_params=_params("parallel", "arbitrary"),
        name="mixer_c_rglru",
    )(pf, pf, cw, cb, wa, ba, wx, bx, lam)


def _block_diag(wb):
    l, g, n, _ = wb.shape
    eye = jnp.eye(g, dtype=wb.dtype)
    return (wb[:, :, :, None, :] * eye[None, :, None, :, None]).reshape(l, g * n, g * n)


def _stick_kernel(q_ref, k_ref, v_ref, o_ref, qs_ref, vs_ref, later_ref, acc_ref):
    step = pl.program_id(1)
    n_pairs = D_HEADS // 2
    n_key_blocks = k_ref.shape[0] // Q_BLOCK
    first_head = lax.broadcasted_iota(jnp.int32, (Q_BLOCK, LANES), 1) < HEAD_DIM

    @pl.when(step == 0)
    def _():
        for jb in range(n_key_blocks):
            for p in range(n_pairs):
                v = v_ref[jb * Q_BLOCK:(jb + 1) * Q_BLOCK, p * LANES:(p + 1) * LANES].astype(F32)
                vs_ref[jb, p, 0:Q_BLOCK, :] = jnp.where(first_head, v, 0.0).astype(BF16)
                vs_ref[jb, p, Q_BLOCK:2 * Q_BLOCK, :] = jnp.where(first_head, 0.0, v).astype(BF16)

    row = lax.broadcasted_iota(jnp.int32, (2 * Q_BLOCK, Q_BLOCK), 0) % Q_BLOCK
    col = lax.broadcasted_iota(jnp.int32, (2 * Q_BLOCK, Q_BLOCK), 1)
    strictly_before = col < row
    rr = lax.broadcasted_iota(jnp.int32, (2 * Q_BLOCK, 2 * Q_BLOCK), 0) % Q_BLOCK
    cc = lax.broadcasted_iota(jnp.int32, (2 * Q_BLOCK, 2 * Q_BLOCK), 1)
    suffix_and_total = jnp.where((rr > cc) | (cc >= Q_BLOCK), 1.0, 0.0).astype(BF16)

    def key_blocks(js, from_diagonal):
        pairs = range(n_pairs)
        blocks = range(len(js))
        ks = [pl.multiple_of(j * Q_BLOCK, Q_BLOCK) for j in js]
        both = [(b, p) for b in blocks for p in pairs]
        z, neg_log1m, log_sig, sums, att = {}, {}, {}, {}, {}
        for b, p in both:
            z[b, p] = _dot_nt(qs_ref[p], k_ref[pl.ds(ks[b], Q_BLOCK), p * LANES:(p + 1) * LANES])
        for b, p in both:
            t = jnp.maximum(z[b, p], 0.0) + jnp.log(1.0 + jnp.exp(-jnp.abs(z[b, p])))
            log_sig[b, p] = z[b, p] - t
            masked = from_diagonal and b == 0
            neg_log1m[b, p] = jnp.where(strictly_before, t, 0.0) if masked else t
        for b, p in both:
            hi, lo = _two_bf16_terms(neg_log1m[b, p])
            sums[b, p] = _dot(jnp.concatenate([hi, lo], axis=1), suffix_and_total)
        tails, totals = {}, []
        for p in pairs:
            later = None if from_diagonal else later_ref[p]
            for b in blocks:
                tail = sums[b, p][:, :Q_BLOCK]
                total = sums[b, p][:, Q_BLOCK:]
                if later is not None:
                    tail = tail + later
                    total = total + later
                tails[b, p] = tail
                later = total
            later_ref[p] = later
            totals.append(later)
        least = jnp.min(jnp.minimum(jnp.minimum(totals[0], totals[1]), jnp.minimum(totals[2], totals[3])))
        for b, p in both:
            a = jnp.exp(log_sig[b, p] - tails[b, p])
            if from_diagonal and b == 0:
                a = jnp.where(strictly_before, a, 0.0)
            a = a.astype(BF16)
            att[b, p] = jnp.concatenate([a[:Q_BLOCK], a[Q_BLOCK:]], axis=1)
        for p in pairs:
            pv = None if from_diagonal else acc_ref[p]
            for b in blocks:
                term = _dot(att[b, p], vs_ref[js[b], p])
                pv = term if pv is None else pv + term
            acc_ref[p] = pv
        return least

    def query_block(sub):
        i = step * D_Q_BLOCKS_PER_STEP + sub
        rows = slice(sub * Q_BLOCK, (sub + 1) * Q_BLOCK)
        q = q_ref[rows, :].astype(F32) * (HEAD_DIM ** -0.5)
        for p in range(n_pairs):
            q_pair = q[:, p * LANES:(p + 1) * LANES]
            qs_ref[p, 0:Q_BLOCK, :] = jnp.where(first_head, q_pair, 0.0).astype(BF16)
            qs_ref[p, Q_BLOCK:2 * Q_BLOCK, :] = jnp.where(first_head, 0.0, q_pair).astype(BF16)

        least = lax.cond(
            i > 1, lambda: key_blocks([i - b for b in range(ALWAYS_VISITED)], True),
            lambda: lax.cond(i > 0, lambda: key_blocks([i, i - 1], True), lambda: key_blocks([i], True)))

        def more(carry):
            jj, least = carry
            return jnp.logical_and(jj < i, least < EXP_UNDERFLOW)

        def earlier(carry):
            jj, _ = carry
            return jj + 1, key_blocks([i - 1 - jj], False)

        lax.while_loop(more, earlier, (jnp.int32(ALWAYS_VISITED - 1), least))

        for p in range(n_pairs):
            o_ref[rows, p * LANES:(p + 1) * LANES] = acc_ref[p].astype(o_ref.dtype)

    for sub in range(D_Q_BLOCKS_PER_STEP):
        query_block(sub)


def _stick(pj):
    b, s, _ = pj.shape
    w = BRANCH_WIDTH
    q_rows = Q_BLOCK * D_Q_BLOCKS_PER_STEP
    return pl.pallas_call(
        _stick_kernel,
        grid=(b, s // q_rows),
        in_specs=[
            pl.BlockSpec((None, q_rows, w), lambda bi, i: (bi, i, QKV_D_Q)),
            pl.BlockSpec((None, s, w), lambda bi, i: (bi, 0, QKV_D_K)),
            pl.BlockSpec((None, s, w), lambda bi, i: (bi, 0, QKV_D_V)),
        ],
        out_specs=pl.BlockSpec((None, q_rows, w), lambda bi, i: (bi, i, 0)),
        out_shape=jax.ShapeDtypeStruct((b, s, w), BF16),
        scratch_shapes=[
            pltpu.VMEM((D_HEADS // 2, 2 * Q_BLOCK, LANES), BF16),
            pltpu.VMEM((s // Q_BLOCK, D_HEADS // 2, 2 * Q_BLOCK, LANES), BF16),
            pltpu.VMEM((D_HEADS // 2, 2 * Q_BLOCK, LANES), F32),
            pltpu.VMEM((D_HEADS // 2, Q_BLOCK, LANES), F32),
        ],
        compiler_params=_params("parallel", "arbitrary"),
        name="mixer_d_stick",
    )(pj, pj, pj)


def _merge_kernel(ya_ref, yb_ref, yc_ref, yd_ref, xn_ref, h_ref, wgate_ref, wb_ref, wo_ref, o_ref):
    d = h_ref.shape[1]
    xn = xn_ref[...]
    mixed = None
    for n, y_ref in enumerate((ya_ref, yb_ref, yc_ref, yd_ref)):
        gate = _sigmoid(_dot(xn, wgate_ref[:, n * d:(n + 1) * d]))
        term = gate * _dot(y_ref[...], wb_ref[n])
        mixed = term if mixed is None else mixed + term
    o_ref[...] = h_ref[...] + _dot(mixed.astype(BF16), wo_ref[...])


def _merge(ya, yb, yc, yd, xn, h, wgate, wb, wo, layer, tm):
    t, d = h.shape
    w = BRANCH_WIDTH
    y_spec = pl.BlockSpec((tm, w), lambda i: (i, 0))
    row = pl.BlockSpec((tm, d), lambda i: (i, 0))
    resident = functools.partial(pl.BlockSpec, pipeline_mode=pl.Buffered(1))
    return pl.pallas_call(
        _merge_kernel,
        grid=(t // tm,),
        in_specs=[
            y_spec, y_spec, y_spec, y_spec, row, row,
            resident((None, d, N_BRANCH * d), lambda i: (layer, 0, 0)),
            resident((None, N_BRANCH, w, d), lambda i: (layer, 0, 0, 0)),
            resident((None, d, d), lambda i: (layer, 0, 0)),
        ],
        out_specs=row,
        out_shape=jax.ShapeDtypeStruct((t, d), F32),
        compiler_params=_params("parallel"),
        name="merge",
    )(ya, yb, yc, yd, xn, h, wgate, wb, wo)


def _ffn_kernel(h_ref, g_ref, wg_ref, wu_ref, wd_ref, gnext_ref, *out_refs, hidden_chunk, last):
    h = h_ref[...]
    hn = _rmsnorm(h, g_ref[...]).astype(BF16)
    hidden = wg_ref.shape[1]
    acc = h
    for c0 in range(0, hidden, hidden_chunk):
        cs = slice(c0, min(c0 + hidden_chunk, hidden))
        gate = _dot(hn, wg_ref[:, cs])
        up = _dot(hn, wu_ref[:, cs])
        act = (gate * _sigmoid(gate) * up).astype(BF16)
        acc = acc + _dot(act, wd_ref[cs, :])
    normed = _rmsnorm(acc, gnext_ref[...])
    if last:
        out_refs[0][...] = normed
    else:
        out_refs[0][...] = acc
        out_refs[1][...] = normed.astype(BF16)


def _ffn(h, g, wg, wu, wd, gnext, layer, next_layer, tm):
    t, d = h.shape
    hidden = wg.shape[-1]
    last = next_layer is None
    row = pl.BlockSpec((tm, d), lambda i: (i, 0))
    resident = functools.partial(pl.BlockSpec, pipeline_mode=pl.Buffered(1))
    if last:
        out_specs, out_shape = row, jax.ShapeDtypeStruct((t, d), F32)
    else:
        out_specs = (row, row)
        out_shape = (jax.ShapeDtypeStruct((t, d), F32), jax.ShapeDtypeStruct((t, d), BF16))
    gnext_row = 0 if last else next_layer
    return pl.pallas_call(
        functools.partial(_ffn_kernel, hidden_chunk=512, last=last),
        grid=(t // tm,),
        in_specs=[
            row,
            pl.BlockSpec((None, 1, d), lambda i: (layer, 0, 0)),
            resident((None, d, hidden), lambda i: (layer, 0, 0)),
            resident((None, d, hidden), lambda i: (layer, 0, 0)),
            resident((None, hidden, d), lambda i: (layer, 0, 0)),
            pl.BlockSpec((None, 1, d), lambda i: (gnext_row, 0, 0)),
        ],
        out_specs=out_specs,
        out_shape=out_shape,
        compiler_params=_params("parallel"),
        name="ffn",
    )(h, g, wg, wu, wd, gnext)


def kernel(x, norm_mix, w_in, a_w_gk, a_b_gk, a_norm, b_rel_bias, c_conv_w, c_conv_b, c_w_a, c_b_a,
           c_w_x, c_b_x, c_lambda, w_branch, w_out, norm_ffn, w_ffn_gate, w_ffn_up, w_ffn_down,
           norm_final):
    bsz, s, d = x.shape
    depth = w_in.shape[0]
    t = bsz * s
    w = BRANCH_WIDTH

    sizes = (A_HEADS * HEAD_DIM, A_HEADS * HEAD_DIM, A_HEADS * A_DV, A_RANK, A_HEADS * A_DV,
             w, w, w, w, w, w, w, w, N_BRANCH * d)
    offs = [0]
    for sz in sizes:
        offs.append(offs[-1] + sz)
    (o_aq, o_ak, o_av, o_ar, o_ag, o_bq, _, _, o_cg, _, o_dq, _, _, o_gate, o_end) = offs

    w_in_b = w_in.astype(BF16)
    w_qkv = jnp.concatenate(
        [w_in_b[:, :, o_av:o_ar], w_in_b[:, :, o_bq:o_cg], w_in_b[:, :, o_dq:o_gate]], axis=-1)
    w_f = jnp.concatenate(
        [w_in_b[:, :, o_ag:o_bq], w_in_b[:, :, o_cg:o_dq], w_in_b[:, :, o_aq:o_av],
         w_in_b[:, :, o_ar:o_ag], jnp.zeros((depth, d, 2 * LANES - A_RANK), BF16)], axis=-1)
    w_gate = w_in_b[:, :, o_gate:o_end]
    wgk = jnp.pad(a_w_gk, ((0, 0), (0, LANES - A_RANK), (0, 0))).astype(BF16)
    vec = lambda p: p.reshape(depth, 1, -1)
    ramp = _bias_ramp(b_rel_bias)
    wa = _block_diag(c_w_a).astype(BF16)
    wx = _block_diag(c_w_x).astype(BF16)
    wb = w_branch.astype(BF16)
    wo = w_out.astype(BF16)
    wg = w_ffn_gate.astype(BF16)
    wu = w_ffn_up.astype(BF16)
    wd = w_ffn_down.astype(BF16)
    gf = norm_final.reshape(1, 1, d)

    h = x.reshape(t, d)
    tm = min(1024, t)
    ts = min(1024, s)
    xn = _norm(h, vec(norm_mix), 0, tm)
    for layer in range(depth):
        pj = _proj(xn, w_qkv, layer, BF16, tm, QKV_COLS).reshape(bsz, s, QKV_COLS)
        pf = _proj(xn, w_f, layer, F32, tm, F_COLS).reshape(bsz, s, F_COLS)
        ya = _gla(pf, pj, wgk, vec(a_b_gk), vec(a_norm), layer, ts)
        yb = _chunk_attn(pj, ramp, layer)
        yc = _rglru(pf, c_conv_w, vec(c_conv_b), wa, vec(c_b_a), wx, vec(c_b_x), vec(c_lambda),
                    layer, ts)
        yd = _stick(pj)
        h = _merge(ya.reshape(t, w), yb.reshape(t, w), yc.reshape(t, w), yd.reshape(t, w),
                   xn, h, w_gate, wb, wo, layer, tm)
        if layer + 1 < depth:
            h, xn = _ffn(h, vec(norm_ffn), wg, wu, wd, vec(norm_mix), layer, layer + 1, tm)
        else:
            h = _ffn(h, vec(norm_ffn), wg, wu, wd, gf, layer, None, tm)
    return h.reshape(bsz, s, d)
```

```python
import functools

import jax
import jax.numpy as jnp
from jax import lax
from jax.experimental import pallas as pl
from jax.experimental.pallas import tpu as pltpu

F32 = jnp.float32
BF16 = jnp.bfloat16

EPS = 1e-6
CHUNK = 64
BRANCH_WIDTH = 512
N_BRANCH = 4
HEAD_DIM = 64
A_HEADS = 4
A_DV = 128
A_RANK = 16
A_GATE_NORM = 16.0
B_HEADS = 8
B_PREV_CHUNKS = 8
B_MAX_REL = 128
C_BLOCKS = 8
C_CONV = 4
C_POW = 8.0
D_HEADS = 8
MASK_VALUE = -1e30
EXP_UNDERFLOW = 104.0
ALWAYS_VISITED = 3

LANES = 128
SUBLANES = 8
VMEM_BYTES_V7X = 64 * 1024 * 1024

QKV_COLS = 7 * BRANCH_WIDTH
QKV_A_V, QKV_B_Q, QKV_B_K, QKV_B_V, QKV_D_Q, QKV_D_K, QKV_D_V = range(7)
F_A_G, F_C_G, F_C_X, F_A_QK = range(4)
F_A_R_COL = 4 * BRANCH_WIDTH
F_COLS = F_A_R_COL + 2 * LANES

Q_BLOCK = 128
B_Q_BLOCKS_PER_STEP = 4
D_Q_BLOCKS_PER_STEP = 4
BAND = (B_PREV_CHUNKS + 2) * CHUNK
B_PAD = B_PREV_CHUNKS * CHUNK
RAMP_WIDTH = BAND + Q_BLOCK


def _dot(a, b):
    return jnp.dot(a, b, preferred_element_type=F32)


def _dot_nt(a, b):
    return lax.dot_general(a, b, (((1,), (1,)), ((), ())), preferred_element_type=F32)


def _dot_tn(a, b):
    return lax.dot_general(a, b, (((0,), (0,)), ((), ())), preferred_element_type=F32)


def _two_bf16_terms(x):
    hi = x.astype(BF16)
    lo = (x - hi.astype(F32)).astype(BF16)
    return hi, lo


def _log1p_exp_neg_abs(z):
    return jnp.log1p(jnp.exp(-jnp.abs(z)))


def _log_sigmoid(z):
    return jnp.minimum(z, 0.0) - _log1p_exp_neg_abs(z)


def _sigmoid(z):
    return 1.0 / (1.0 + jnp.exp(-z))


VMEM_REQUEST_BYTES = VMEM_BYTES_V7X - 8 * 1024 * 1024


def _params(*semantics):
    return pltpu.CompilerParams(dimension_semantics=semantics, vmem_limit_bytes=VMEM_REQUEST_BYTES)


def _rmsnorm(x, g):
    return x * lax.rsqrt(jnp.mean(x * x, axis=-1, keepdims=True) + EPS) * g


def _norm_kernel(x_ref, g_ref, o_ref):
    o_ref[...] = _rmsnorm(x_ref[...], g_ref[...]).astype(o_ref.dtype)


def _norm(x, g, layer, tm):
    t, d = x.shape
    return pl.pallas_call(
        _norm_kernel,
        grid=(t // tm,),
        in_specs=[
            pl.BlockSpec((tm, d), lambda i: (i, 0)),
            pl.BlockSpec((None, 1, d), lambda i: (layer, 0, 0)),
        ],
        out_specs=pl.BlockSpec((tm, d), lambda i: (i, 0)),
        out_shape=jax.ShapeDtypeStruct((t, d), BF16),
        compiler_params=_params("parallel"),
        name="norm",
    )(x, g)


def _proj_kernel(x_ref, w_ref, o_ref):
    o_ref[...] = _dot(x_ref[...], w_ref[...]).astype(o_ref.dtype)


def _proj(xn, w, layer, out_dtype, tm, tn):
    t, d = xn.shape
    n = w.shape[-1]
    return pl.pallas_call(
        _proj_kernel,
        grid=(t // tm, n // tn),
        in_specs=[
            pl.BlockSpec((tm, d), lambda i, j: (i, 0)),
            pl.BlockSpec((None, d, tn), lambda i, j: (layer, 0, j)),
        ],
        out_specs=pl.BlockSpec((tm, tn), lambda i, j: (i, j)),
        out_shape=jax.ShapeDtypeStruct((t, n), out_dtype),
        compiler_params=_params("parallel", "arbitrary"),
        name="proj",
    )(xn, w)


def _gla_kernel(qk_ref, v_ref, g_ref, r_ref, wgk_ref, bgk_ref, ng_ref, o_ref, state_ref, *,
                n_chunks):
    dk_all = A_HEADS * HEAD_DIM
    chunks = range(n_chunks)
    rows = [slice(c * CHUNK, (c + 1) * CHUNK) for c in chunks]

    @pl.when(pl.program_id(1) == 0)
    def _():
        state_ref[...] = jnp.zeros_like(state_ref)

    tri2 = jnp.where(lax.broadcasted_iota(jnp.int32, (CHUNK, 2 * CHUNK), 0)
                     >= lax.broadcasted_iota(jnp.int32, (CHUNK, 2 * CHUNK), 1) % CHUNK,
                     1.0, 0.0).astype(BF16)
    own_head = (lax.broadcasted_iota(jnp.int32, (dk_all, dk_all), 0) // HEAD_DIM
                == lax.broadcasted_iota(jnp.int32, (dk_all, dk_all), 1) // HEAD_DIM)

    def head_stacked(x):
        return jnp.where(own_head, jnp.concatenate([x] * A_HEADS, axis=0), 0.0).astype(BF16)

    r = r_ref[...].astype(BF16)
    gk = _log_sigmoid(_dot(r, wgk_ref[...]) + bgk_ref[...]) * (1.0 / A_GATE_NORM)
    gk_hi, gk_lo = _two_bf16_terms(gk)
    k = qk_ref[:, dk_all:]
    q = qk_ref[:, :dk_all] * (HEAD_DIM ** -0.5)

    k_dec, q_st, v_st, decay = [], [], [], []
    for c in chunks:
        cum = _dot(tri2, jnp.concatenate([gk_hi[rows[c]], gk_lo[rows[c]]], axis=0))
        tot = cum[CHUNK - 1:CHUNK, :]
        k_dec.append(head_stacked(k[rows[c]] * jnp.exp(tot - cum)))
        q_st.append(head_stacked(q[rows[c]]))
        v_c = v_ref[rows[c], :]
        v_st.append(jnp.concatenate([v_c[:, h * A_DV:(h + 1) * A_DV] for h in range(A_HEADS)], axis=0))
        decay.append(jnp.exp(tot))
    kv = [_dot_tn(v_st[c], k_dec[c]) for c in chunks]

    state = state_ref[...]
    outs = []
    for c in chunks:
        state = state * decay[c] + kv[c]
        outs.append(_dot_nt(q_st[c], state.astype(BF16)))
    state_ref[...] = state

    for h in range(A_HEADS):
        cols = slice(h * A_DV, (h + 1) * A_DV)
        o = jnp.concatenate([outs[c][h * CHUNK:(h + 1) * CHUNK] for c in chunks], axis=0)
        o = o * lax.rsqrt(jnp.mean(o * o, axis=-1, keepdims=True) + EPS) * ng_ref[...]
        g = g_ref[:, cols]
        o_ref[:, cols] = (o * (g * _sigmoid(g))).astype(o_ref.dtype)


def _gla(pf, pj, wgk, bgk, ng, layer, tq):
    b, s, _ = pf.shape
    w = BRANCH_WIDTH
    return pl.pallas_call(
        functools.partial(_gla_kernel, n_chunks=tq // CHUNK),
        grid=(b, s // tq),
        in_specs=[
            pl.BlockSpec((None, tq, w), lambda bi, i: (bi, i, F_A_QK)),
            pl.BlockSpec((None, tq, w), lambda bi, i: (bi, i, QKV_A_V)),
            pl.BlockSpec((None, tq, w), lambda bi, i: (bi, i, F_A_G)),
            pl.BlockSpec((None, tq, LANES), lambda bi, i: (bi, i, F_A_R_COL // LANES)),
            pl.BlockSpec((None, LANES, A_HEADS * HEAD_DIM), lambda bi, i: (layer, 0, 0)),
            pl.BlockSpec((None, 1, A_HEADS * HEAD_DIM), lambda bi, i: (layer, 0, 0)),
            pl.BlockSpec((None, 1, A_DV), lambda bi, i: (layer, 0, 0)),
        ],
        out_specs=pl.BlockSpec((None, tq, w), lambda bi, i: (bi, i, 0)),
        out_shape=jax.ShapeDtypeStruct((b, s, w), BF16),
        scratch_shapes=[pltpu.VMEM((A_DV, A_HEADS * HEAD_DIM), F32)],
        compiler_params=_params("parallel", "arbitrary"),
        name="mixer_a_gla",
    )(pf, pj, pf, pf, wgk, bgk, ng)


def _chunk_attn_kernel(q_ref, k_ref, v_ref, ramp_ref, o_ref, qs_ref, bias_ref, kpad_ref, vfirst_ref,
                       vsecond_ref):
    step = pl.program_id(1)
    s = k_ref.shape[0]
    n_pairs = B_HEADS // 2
    init_rows = 2 * Q_BLOCK

    @pl.when(step == 0)
    def _():
        zeros = jnp.zeros((B_PAD, BRANCH_WIDTH), BF16)
        kpad_ref[0:B_PAD, :] = zeros
        vfirst_ref[0:B_PAD, :] = zeros
        vsecond_ref[0:B_PAD, :] = zeros
        kpad_ref[B_PAD:B_PAD + s, :] = k_ref[...]
        lane = lax.broadcasted_iota(jnp.int32, (init_rows, BRANCH_WIDTH), 1)
        first = (lane % LANES) < HEAD_DIM
        for r0 in range(0, s, init_rows):
            v = v_ref[r0:r0 + init_rows, :].astype(F32)
            vfirst_ref[B_PAD + r0:B_PAD + r0 + init_rows, :] = jnp.where(first, v, 0.0).astype(BF16)
            vsecond_ref[B_PAD + r0:B_PAD + r0 + init_rows, :] = jnp.where(first, 0.0, v).astype(BF16)
        qi = lax.broadcasted_iota(jnp.int32, (Q_BLOCK, BAND), 0) // CHUNK
        kj = lax.broadcasted_iota(jnp.int32, (Q_BLOCK, BAND), 1) // CHUNK
        in_band = (kj >= qi) & (kj <= qi + B_PREV_CHUNKS)
        for h in range(B_HEADS):
            ramp = jnp.broadcast_to(ramp_ref[h:h + 1, :], (Q_BLOCK, RAMP_WIDTH))
            skew = pltpu.roll(ramp, RAMP_WIDTH - (Q_BLOCK - 1), axis=1, stride=1, stride_axis=0)
            half = slice((h % 2) * Q_BLOCK, (h % 2 + 1) * Q_BLOCK)
            bias_ref[h // 2, half, :] = jnp.where(in_band, skew[:, :BAND], MASK_VALUE)

    first_head = lax.broadcasted_iota(jnp.int32, (Q_BLOCK, LANES), 1) < HEAD_DIM

    def attend(sub, has_padding):
        i = step * B_Q_BLOCKS_PER_STEP + sub
        rows = slice(sub * Q_BLOCK, (sub + 1) * Q_BLOCK)
        q = q_ref[rows, :].astype(F32) * (HEAD_DIM ** -0.5)
        for p in range(n_pairs):
            q_pair = q[:, p * LANES:(p + 1) * LANES]
            qs_ref[p, 0:Q_BLOCK, :] = jnp.where(first_head, q_pair, 0.0).astype(BF16)
            qs_ref[p, Q_BLOCK:2 * Q_BLOCK, :] = jnp.where(first_head, 0.0, q_pair).astype(BF16)
        window = pl.ds(pl.multiple_of(i * Q_BLOCK, Q_BLOCK), BAND)
        pairs = range(n_pairs)
        cols = [slice(p * LANES, (p + 1) * LANES) for p in pairs]
        sc = [_dot_nt(qs_ref[p], kpad_ref[window, cols[p]]) + bias_ref[p] for p in pairs]
        if has_padding:
            in_seq = lax.broadcasted_iota(jnp.int32, (2 * Q_BLOCK, BAND), 1) >= B_PAD - i * Q_BLOCK
            sc = [jnp.where(in_seq, x, MASK_VALUE) for x in sc]
        e = [jnp.exp(x - jnp.max(x, axis=-1, keepdims=True)) for x in sc]
        prob = [(x * (1.0 / jnp.sum(x, axis=-1, keepdims=True))).astype(BF16) for x in e]
        for p in pairs:
            both = jnp.concatenate([prob[p][:Q_BLOCK], prob[p][Q_BLOCK:]], axis=1)
            values = jnp.concatenate([vfirst_ref[window, cols[p]], vsecond_ref[window, cols[p]]], axis=0)
            o_ref[rows, cols[p]] = _dot(both, values).astype(o_ref.dtype)

    padded_steps = B_PAD // (Q_BLOCK * B_Q_BLOCKS_PER_STEP)

    @pl.when(step < padded_steps)
    def _():
        for sub in range(B_Q_BLOCKS_PER_STEP):
            attend(sub, True)

    @pl.when(step >= padded_steps)
    def _():
        for sub in range(B_Q_BLOCKS_PER_STEP):
            attend(sub, False)


def _chunk_attn(pj, ramp, layer):
    b, s, _ = pj.shape
    w = BRANCH_WIDTH
    q_rows = Q_BLOCK * B_Q_BLOCKS_PER_STEP
    return pl.pallas_call(
        _chunk_attn_kernel,
        grid=(b, s // q_rows),
        in_specs=[
            pl.BlockSpec((None, q_rows, w), lambda bi, i: (bi, i, QKV_B_Q)),
            pl.BlockSpec((None, s, w), lambda bi, i: (bi, 0, QKV_B_K)),
            pl.BlockSpec((None, s, w), lambda bi, i: (bi, 0, QKV_B_V)),
            pl.BlockSpec((None, B_HEADS, RAMP_WIDTH), lambda bi, i: (layer, 0, 0)),
        ],
        out_specs=pl.BlockSpec((None, q_rows, w), lambda bi, i: (bi, i, 0)),
        out_shape=jax.ShapeDtypeStruct((b, s, w), BF16),
        scratch_shapes=[
            pltpu.VMEM((B_HEADS // 2, 2 * Q_BLOCK, LANES), BF16),
            pltpu.VMEM((B_HEADS // 2, 2 * Q_BLOCK, BAND), F32),
            pltpu.VMEM((s + B_PAD, w), BF16),
            pltpu.VMEM((s + B_PAD, w), BF16),
            pltpu.VMEM((s + B_PAD, w), BF16),
        ],
        compiler_params=_params("parallel", "arbitrary"),
        name="mixer_b_chunk_attn",
    )(pj, pj, pj, ramp)


def _bias_ramp(rel_table):
    idx = jnp.clip((BAND - 1) - jnp.arange(RAMP_WIDTH), -B_MAX_REL, B_MAX_REL) + B_MAX_REL
    return rel_table.astype(F32)[:, :, idx]


def _rglru_kernel(g_ref, x_ref, cw_ref, cb_ref, wa_ref, ba_ref, wx_ref, bx_ref, lam_ref, o_ref,
                  xe_ref, a_ref, b_ref, h_ref, carry_ref):
    ts = x_ref.shape[0]
    halo = SUBLANES

    @pl.when(pl.program_id(1) == 0)
    def _():
        xe_ref[0:halo, :] = jnp.zeros((halo, BRANCH_WIDTH), F32)
        carry_ref[...] = jnp.zeros_like(carry_ref)

    x = x_ref[...]
    xe_ref[halo:halo + ts, :] = x
    xc = cb_ref[...]
    for j in range(C_CONV - 1):
        lag = C_CONV - 1 - j
        xc = xc + xe_ref[halo - lag:halo - lag + ts, :] * cw_ref[j:j + 1, :]
    xc = xc + x * cw_ref[C_CONV - 1:C_CONV, :]
    xe_ref[0:halo, :] = xe_ref[ts:ts + halo, :]

    xcb = xc.astype(BF16)
    r = _sigmoid(_dot(xcb, wa_ref[...]) + ba_ref[...])
    gate_i = _sigmoid(_dot(xcb, wx_ref[...]) + bx_ref[...])
    lam = lam_ref[...]
    softplus_neg_lam = jnp.maximum(-lam, 0.0) + _log1p_exp_neg_abs(lam)
    log_a = -C_POW * r * softplus_neg_lam
    a = jnp.exp(log_a)
    a_ref[...] = a
    b_ref[...] = jnp.sqrt(-jnp.tanh(log_a) * (a * a + 1.0)) * (gate_i * xc)

    sub = lax.broadcasted_iota(jnp.int32, (SUBLANES, BRANCH_WIDTH), 0)

    def group(gi, carry):
        rows = pl.ds(pl.multiple_of(gi * SUBLANES, SUBLANES), SUBLANES)
        a = a_ref[rows, :]
        bb = b_ref[rows, :]
        for d in (1, 2, 4):
            a_prev = pltpu.roll(a, d, axis=0)
            b_prev = pltpu.roll(bb, d, axis=0)
            has_prev = sub >= d
            bb = jnp.where(has_prev, a * b_prev + bb, bb)
            a = jnp.where(has_prev, a * a_prev, a)
        h = a * carry + bb
        h_ref[rows, :] = h
        return jnp.broadcast_to(h[SUBLANES - 1:SUBLANES, :], (SUBLANES, BRANCH_WIDTH))

    carry_ref[...] = lax.fori_loop(0, ts // SUBLANES, group, carry_ref[...], unroll=8)
    o_ref[...] = (jax.nn.gelu(g_ref[...]) * h_ref[...]).astype(o_ref.dtype)


def _rglru(pf, cw, cb, wa, ba, wx, bx, lam, layer, ts):
    b, s, _ = pf.shape
    w = BRANCH_WIDTH
    vec = pl.BlockSpec((None, 1, w), lambda bi, i: (layer, 0, 0))
    mat = pl.BlockSpec((None, w, w), lambda bi, i: (layer, 0, 0))
    return pl.pallas_call(
        _rglru_kernel,
        grid=(b, s // ts),
        in_specs=[
            pl.BlockSpec((None, ts, w), lambda bi, i: (bi, i, F_C_G)),
            pl.BlockSpec((None, ts, w), lambda bi, i: (bi, i, F_C_X)),
            pl.BlockSpec((None, C_CONV, w), lambda bi, i: (layer, 0, 0)),
            vec, mat, vec, mat, vec, vec,
        ],
        out_specs=pl.BlockSpec((None, ts, w), lambda bi, i: (bi, i, 0)),
        out_shape=jax.ShapeDtypeStruct((b, s, w), BF16),
        scratch_shapes=[
            pltpu.VMEM((ts + 2 * SUBLANES, w), F32),
            pltpu.VMEM((ts, w), F32),
            pltpu.VMEM((ts, w), F32),
            pltpu.VMEM((ts, w), F32),
            pltpu.VMEM((SUBLANES, w), F32),
        ],
        compiler_params=_params("parallel", "arbitrary"),
        name="mixer_c_rglru",
    )(pf, pf, cw, cb, wa, ba, wx, bx, lam)


def _block_diag(wb):
    l, g, n, _ = wb.shape
    eye = jnp.eye(g, dtype=wb.dtype)
    return (wb[:, :, :, None, :] * eye[None, :, None, :, None]).reshape(l, g * n, g * n)


def _stick_kernel(q_ref, k_ref, v_ref, o_ref, qs_ref, vs_ref, later_ref, acc_ref):
    step = pl.program_id(1)
    n_pairs = D_HEADS // 2
    n_key_blocks = k_ref.shape[0] // Q_BLOCK
    first_head = lax.broadcasted_iota(jnp.int32, (Q_BLOCK, LANES), 1) < HEAD_DIM

    @pl.when(step == 0)
    def _():
        for jb in range(n_key_blocks):
            for p in range(n_pairs):
                v = v_ref[jb * Q_BLOCK:(jb + 1) * Q_BLOCK, p * LANES:(p + 1) * LANES].astype(F32)
                vs_ref[jb, p, 0:Q_BLOCK, :] = jnp.where(first_head, v, 0.0).astype(BF16)
                vs_ref[jb, p, Q_BLOCK:2 * Q_BLOCK, :] = jnp.where(first_head, 0.0, v).astype(BF16)

    row = lax.broadcasted_iota(jnp.int32, (2 * Q_BLOCK, Q_BLOCK), 0) % Q_BLOCK
    col = lax.broadcasted_iota(jnp.int32, (2 * Q_BLOCK, Q_BLOCK), 1)
    strictly_before = col < row
    rr = lax.broadcasted_iota(jnp.int32, (2 * Q_BLOCK, 2 * Q_BLOCK), 0) % Q_BLOCK
    cc = lax.broadcasted_iota(jnp.int32, (2 * Q_BLOCK, 2 * Q_BLOCK), 1)
    suffix_and_total = jnp.where((rr > cc) | (cc >= Q_BLOCK), 1.0, 0.0).astype(BF16)

    def key_blocks(js, from_diagonal):
        pairs = range(n_pairs)
        blocks = range(len(js))
        ks = [pl.multiple_of(j * Q_BLOCK, Q_BLOCK) for j in js]
        both = [(b, p) for b in blocks for p in pairs]
        z, neg_log1m, log_sig, sums, att = {}, {}, {}, {}, {}
        for b, p in both:
            z[b, p] = _dot_nt(qs_ref[p], k_ref[pl.ds(ks[b], Q_BLOCK), p * LANES:(p + 1) * LANES])
        for b, p in both:
            t = jnp.maximum(z[b, p], 0.0) + jnp.log(1.0 + jnp.exp(-jnp.abs(z[b, p])))
            log_sig[b, p] = z[b, p] - t
            masked = from_diagonal and b == 0
            neg_log1m[b, p] = jnp.where(strictly_before, t, 0.0) if masked else t
        for b, p in both:
            hi, lo = _two_bf16_terms(neg_log1m[b, p])
            sums[b, p] = _dot(jnp.concatenate([hi, lo], axis=1), suffix_and_total)
        tails, totals = {}, []
        for p in pairs:
            later = None if from_diagonal else later_ref[p]
            for b in blocks:
                tail = sums[b, p][:, :Q_BLOCK]
                total = sums[b, p][:, Q_BLOCK:]
                if later is not None:
                    tail = tail + later
                    total = total + later
                tails[b, p] = tail
                later = total
            later_ref[p] = later
            totals.append(later)
        least = jnp.min(jnp.minimum(jnp.minimum(totals[0], totals[1]), jnp.minimum(totals[2], totals[3])))
        for b, p in both:
            a = jnp.exp(log_sig[b, p] - tails[b, p])
            if from_diagonal and b == 0:
                a = jnp.where(strictly_before, a, 0.0)
            a = a.astype(BF16)
            att[b, p] = jnp.concatenate([a[:Q_BLOCK], a[Q_BLOCK:]], axis=1)
        for p in pairs:
            pv = None if from_diagonal else acc_ref[p]
            for b in blocks:
                term = _dot(att[b, p], vs_ref[js[b], p])
                pv = term if pv is None else pv + term
            acc_ref[p] = pv
        return least

    def query_block(sub):
        i = step * D_Q_BLOCKS_PER_STEP + sub
        rows = slice(sub * Q_BLOCK, (sub + 1) * Q_BLOCK)
        q = q_ref[rows, :].astype(F32) * (HEAD_DIM ** -0.5)
        for p in range(n_pairs):
            q_pair = q[:, p * LANES:(p + 1) * LANES]
            qs_ref[p, 0:Q_BLOCK, :] = jnp.where(first_head, q_pair, 0.0).astype(BF16)
            qs_ref[p, Q_BLOCK:2 * Q_BLOCK, :] = jnp.where(first_head, 0.0, q_pair).astype(BF16)

        least = lax.cond(
            i > 1, lambda: key_blocks([i - b for b in range(ALWAYS_VISITED)], True),
            lambda: lax.cond(i > 0, lambda: key_blocks([i, i - 1], True), lambda: key_blocks([i], True)))

        def more(carry):
            jj, least = carry
            return jnp.logical_and(jj < i, least < EXP_UNDERFLOW)

        def earlier(carry):
            jj, _ = carry
            return jj + 1, key_blocks([i - 1 - jj], False)

        lax.while_loop(more, earlier, (jnp.int32(ALWAYS_VISITED - 1), least))

        for p in range(n_pairs):
            o_ref[rows, p * LANES:(p + 1) * LANES] = acc_ref[p].astype(o_ref.dtype)

    for sub in range(D_Q_BLOCKS_PER_STEP):
        query_block(sub)


def _stick(pj):
    b, s, _ = pj.shape
    w = BRANCH_WIDTH
    q_rows = Q_BLOCK * D_Q_BLOCKS_PER_STEP
    return pl.pallas_call(
        _stick_kernel,
        grid=(b, s // q_rows),
        in_specs=[
            pl.BlockSpec((None, q_rows, w), lambda bi, i: (bi, i, QKV_D_Q)),
            pl.BlockSpec((None, s, w), lambda bi, i: (bi, 0, QKV_D_K)),
            pl.BlockSpec((None, s, w), lambda bi, i: (bi, 0, QKV_D_V)),
        ],
        out_specs=pl.BlockSpec((None, q_rows, w), lambda bi, i: (bi, i, 0)),
        out_shape=jax.ShapeDtypeStruct((b, s, w), BF16),
        scratch_shapes=[
            pltpu.VMEM((D_HEADS // 2, 2 * Q_BLOCK, LANES), BF16),
            pltpu.VMEM((s // Q_BLOCK, D_HEADS // 2, 2 * Q_BLOCK, LANES), BF16),
            pltpu.VMEM((D_HEADS // 2, 2 * Q_BLOCK, LANES), F32),
            pltpu.VMEM((D_HEADS // 2, Q_BLOCK, LANES), F32),
        ],
        compiler_params=_params("parallel", "arbitrary"),
        name="mixer_d_stick",
    )(pj, pj, pj)


def _merge_kernel(ya_ref, yb_ref, yc_ref, yd_ref, xn_ref, h_ref, wgate_ref, wb_ref, wo_ref, o_ref):
    d = h_ref.shape[1]
    xn = xn_ref[...]
    mixed = None
    for n, y_ref in enumerate((ya_ref, yb_ref, yc_ref, yd_ref)):
        gate = _sigmoid(_dot(xn, wgate_ref[:, n * d:(n + 1) * d]))
        term = gate * _dot(y_ref[...], wb_ref[n])
        mixed = term if mixed is None else mixed + term
    o_ref[...] = h_ref[...] + _dot(mixed.astype(BF16), wo_ref[...])


def _merge(ya, yb, yc, yd, xn, h, wgate, wb, wo, layer, tm):
    t, d = h.shape
    w = BRANCH_WIDTH
    y_spec = pl.BlockSpec((tm, w), lambda i: (i, 0))
    row = pl.BlockSpec((tm, d), lambda i: (i, 0))
    resident = functools.partial(pl.BlockSpec, pipeline_mode=pl.Buffered(1))
    return pl.pallas_call(
        _merge_kernel,
        grid=(t // tm,),
        in_specs=[
            y_spec, y_spec, y_spec, y_spec, row, row,
            resident((None, d, N_BRANCH * d), lambda i: (layer, 0, 0)),
            resident((None, N_BRANCH, w, d), lambda i: (layer, 0, 0, 0)),
            resident((None, d, d), lambda i: (layer, 0, 0)),
        ],
        out_specs=row,
        out_shape=jax.ShapeDtypeStruct((t, d), F32),
        compiler_params=_params("parallel"),
        name="merge",
    )(ya, yb, yc, yd, xn, h, wgate, wb, wo)


def _ffn_kernel(h_ref, g_ref, wg_ref, wu_ref, wd_ref, gnext_ref, *out_refs, hidden_chunk, last):
    h = h_ref[...]
    hn = _rmsnorm(h, g_ref[...]).astype(BF16)
    hidden = wg_ref.shape[1]
    acc = h
    for c0 in range(0, hidden, hidden_chunk):
        cs = slice(c0, min(c0 + hidden_chunk, hidden))
        gate = _dot(hn, wg_ref[:, cs])
        up = _dot(hn, wu_ref[:, cs])
        act = (gate * _sigmoid(gate) * up).astype(BF16)
        acc = acc + _dot(act, wd_ref[cs, :])
    normed = _rmsnorm(acc, gnext_ref[...])
    if last:
        out_refs[0][...] = normed
    else:
        out_refs[0][...] = acc
        out_refs[1][...] = normed.astype(BF16)


def _ffn(h, g, wg, wu, wd, gnext, layer, next_layer, tm):
    t, d = h.shape
    hidden = wg.shape[-1]
    last = next_layer is None
    row = pl.BlockSpec((tm, d), lambda i: (i, 0))
    resident = functools.partial(pl.BlockSpec, pipeline_mode=pl.Buffered(1))
    if last:
        out_specs, out_shape = row, jax.ShapeDtypeStruct((t, d), F32)
    else:
        out_specs = (row, row)
        out_shape = (jax.ShapeDtypeStruct((t, d), F32), jax.ShapeDtypeStruct((t, d), BF16))
    gnext_row = 0 if last else next_layer
    return pl.pallas_call(
        functools.partial(_ffn_kernel, hidden_chunk=512, last=last),
        grid=(t // tm,),
        in_specs=[
            row,
            pl.BlockSpec((None, 1, d), lambda i: (layer, 0, 0)),
            resident((None, d, hidden), lambda i: (layer, 0, 0)),
            resident((None, d, hidden), lambda i: (layer, 0, 0)),
            resident((None, hidden, d), lambda i: (layer, 0, 0)),
            pl.BlockSpec((None, 1, d), lambda i: (gnext_row, 0, 0)),
        ],
        out_specs=out_specs,
        out_shape=out_shape,
        compiler_params=_params("parallel"),
        name="ffn",
    )(h, g, wg, wu, wd, gnext)


def kernel(x, norm_mix, w_in, a_w_gk, a_b_gk, a_norm, b_rel_bias, c_conv_w, c_conv_b, c_w_a, c_b_a,
           c_w_x, c_b_x, c_lambda, w_branch, w_out, norm_ffn, w_ffn_gate, w_ffn_up, w_ffn_down,
           norm_final):
    bsz, s, d = x.shape
    depth = w_in.shape[0]
    t = bsz * s
    w = BRANCH_WIDTH

    sizes = (A_HEADS * HEAD_DIM, A_HEADS * HEAD_DIM, A_HEADS * A_DV, A_RANK, A_HEADS * A_DV,
             w, w, w, w, w, w, w, w, N_BRANCH * d)
    offs = [0]
    for sz in sizes:
        offs.append(offs[-1] + sz)
    (o_aq, o_ak, o_av, o_ar, o_ag, o_bq, _, _, o_cg, _, o_dq, _, _, o_gate, o_end) = offs

    w_in_b = w_in.astype(BF16)
    w_qkv = jnp.concatenate(
        [w_in_b[:, :, o_av:o_ar], w_in_b[:, :, o_bq:o_cg], w_in_b[:, :, o_dq:o_gate]], axis=-1)
    w_f = jnp.concatenate(
        [w_in_b[:, :, o_ag:o_bq], w_in_b[:, :, o_cg:o_dq], w_in_b[:, :, o_aq:o_av],
         w_in_b[:, :, o_ar:o_ag], jnp.zeros((depth, d, 2 * LANES - A_RANK), BF16)], axis=-1)
    w_gate = w_in_b[:, :, o_gate:o_end]
    wgk = jnp.pad(a_w_gk, ((0, 0), (0, LANES - A_RANK), (0, 0))).astype(BF16)
    vec = lambda p: p.reshape(depth, 1, -1)
    ramp = _bias_ramp(b_rel_bias)
    wa = _block_diag(c_w_a).astype(BF16)
    wx = _block_diag(c_w_x).astype(BF16)
    wb = w_branch.astype(BF16)
    wo = w_out.astype(BF16)
    wg = w_ffn_gate.astype(BF16)
    wu = w_ffn_up.astype(BF16)
    wd = w_ffn_down.astype(BF16)
    gf = norm_final.reshape(1, 1, d)

    h = x.reshape(t, d)
    tm = min(1024, t)
    ts = min(1024, s)
    xn = _norm(h, vec(norm_mix), 0, tm)
    for layer in range(depth):
        pj = _proj(xn, w_qkv, layer, BF16, tm, QKV_COLS).reshape(bsz, s, QKV_COLS)
        pf = _proj(xn, w_f, layer, F32, tm, F_COLS).reshape(bsz, s, F_COLS)
        ya = _gla(pf, pj, wgk, vec(a_b_gk), vec(a_norm), layer, ts)
        yb = _chunk_attn(pj, ramp, layer)
        yc = _rglru(pf, c_conv_w, vec(c_conv_b), wa, vec(c_b_a), wx, vec(c_b_x), vec(c_lambda),
                    layer, ts)
        yd = _stick(pj)
        h = _merge(ya.reshape(t, w), yb.reshape(t, w), yc.reshape(t, w), yd.reshape(t, w),
                   xn, h, w_gate, wb, wo, layer, tm)
        if layer + 1 < depth:
            h, xn = _ffn(h, vec(norm_ffn), wg, wu, wd, vec(norm_mix), layer, layer + 1, tm)
        else:
            h = _ffn(h, vec(norm_ffn), wg, wu, wd, gf, layer, None, tm)
    return h.reshape(bsz, s, d)
```
